```python
import jax, jax.numpy as jnp
from jax import lax
import numpy as np

D_MODEL = 1024
BATCH = 2
SEQ = 8192
DEPTH = 2
DEC_BATCH = 128
DEC_SEQ = 1
PAST_LEN = 2048
PAGE_SIZE = 128

HA = 4
DA = 128
WA = HA * DA
HB = 4
DKB = 64
DVB = 64
WB = HB * DVB
DC = 256
CONV_W = 31
D_FF = 2816
FFN_CONV_W = 3
Q_BLOCK = 128
RET_CHUNK = 128
ROPE_BASE = 10000.0
EPS = 1e-6
COL_SIZES = (WA, WA, WA, HA, HB * DKB, HB * DKB, WB, WB, DC, DC)
N_IN = sum(COL_SIZES)

kernel_name = 'hybrid_fox_retnet_conformer_decoder_step'

F32 = jnp.float32


def _standardize(x):
    xf = x.astype(F32)
    mu = jnp.mean(xf, axis=-1, keepdims=True)
    xc = xf - mu
    return xc * lax.rsqrt(jnp.mean(xc * xc, axis=-1, keepdims=True) + EPS)


def rmsnorm(x, g):
    xf = x.astype(F32)
    y = xf * lax.rsqrt(jnp.mean(xf * xf, axis=-1, keepdims=True) + EPS)
    return (y * g.astype(F32)).astype(x.dtype)


def layernorm(x, g, b):
    return (_standardize(x) * g.astype(F32) + b.astype(F32)).astype(x.dtype)


def split_cols(p):
    offs = np.cumsum(COL_SIZES)[:-1].tolist()
    return jnp.split(p, offs, axis=-1)


def rope(x, pos):
    half = x.shape[-1] // 2
    inv = ROPE_BASE ** (-jnp.arange(half, dtype=F32) / half)
    ang = pos.astype(F32)[:, None] * inv[None, :]
    cos = jnp.cos(ang)[None, :, None, :]
    sin = jnp.sin(ang)[None, :, None, :]
    xf = x.astype(F32)
    x1, x2 = xf[..., :half], xf[..., half:]
    return jnp.concatenate([x1 * cos - x2 * sin, x1 * sin + x2 * cos], axis=-1).astype(x.dtype)


def causal_dwconv(ctx, u, w, b):
    ext = jnp.concatenate([ctx.astype(u.dtype), u], axis=1)
    y = lax.conv_general_dilated(ext, w[:, None, :].astype(u.dtype), window_strides=(1,), padding='VALID',
                                 dimension_numbers=('NWC', 'WIO', 'NWC'), feature_group_count=u.shape[-1])
    return y + b.astype(u.dtype), ext[:, -(w.shape[0] - 1):]


def fox_prompt(q, k, v, logf):
    n, s = q.shape[0], q.shape[1]
    nb = s // Q_BLOCK
    scale = DA ** -0.5
    cum = jnp.cumsum(logf, axis=1).transpose(0, 2, 1)
    qb = q.reshape(n, nb, Q_BLOCK, HA, DA).transpose(1, 0, 2, 3, 4)
    cqb = cum.reshape(n, HA, nb, Q_BLOCK).transpose(2, 0, 1, 3)
    kpos = jnp.arange(s)

    def block(args):
        bi, qi, ci = args
        qpos = bi * Q_BLOCK + jnp.arange(Q_BLOCK)
        logits = jnp.einsum('nqhd,nkhd->nhqk', qi, k).astype(F32) * scale
        logits = logits + ci[..., :, None] - cum[:, :, None, :]
        logits = jnp.where(kpos[None, :] <= qpos[:, None], logits, -jnp.inf)
        p = jax.nn.softmax(logits, axis=-1).astype(v.dtype)
        return jnp.einsum('nhqk,nkhd->nqhd', p, v)

    out = lax.map(block, (jnp.arange(nb), qb, cqb))
    return out.transpose(1, 0, 2, 3, 4).reshape(n, s, HA, DA)


def fox_sample(q, k, v, logf, pool_k, pool_v, pool_lf, page_table):
    n, t = q.shape[0], q.shape[1]
    past = page_table.shape[1] * PAGE_SIZE
    scale = DA ** -0.5
    k_all = jnp.concatenate([pool_k[page_table].reshape(n, past, HA, DA).astype(k.dtype), k], axis=1)
    v_all = jnp.concatenate([pool_v[page_table].reshape(n, past, HA, DA).astype(v.dtype), v], axis=1)
    lf_all = jnp.concatenate([pool_lf[page_table].reshape(n, past, HA).astype(F32), logf], axis=1)
    cum = jnp.cumsum(lf_all, axis=1).transpose(0, 2, 1)
    logits = jnp.einsum('nqhd,nkhd->nhqk', q, k_all).astype(F32) * scale
    logits = logits + cum[:, :, past:, None] - cum[:, :, None, :]
    kpos = jnp.arange(past + t)
    qpos = past + jnp.arange(t)
    logits = jnp.where(kpos[None, :] <= qpos[:, None], logits, -jnp.inf)
    p = jax.nn.softmax(logits, axis=-1).astype(v.dtype)
    return jnp.einsum('nhqk,nkhd->nqhd', p, v_all)


def retention(q, k, v, s0, chunk):
    n, L = q.shape[0], q.shape[1]
    nc = L // chunk
    log_g = jnp.log1p(-(2.0 ** (-5.0 - jnp.arange(HB, dtype=F32))))
    idx = jnp.arange(chunk, dtype=F32)
    rel = idx[:, None] - idx[None, :]
    decay = jnp.where(rel >= 0, jnp.exp(log_g[:, None, None] * jnp.maximum(rel, 0.0)[None]), 0.0)
    xi = jnp.exp(log_g[:, None] * (idx + 1.0)[None])
    zeta = jnp.exp(log_g[:, None] * (chunk - 1.0 - idx)[None])
    g_chunk = jnp.exp(log_g * chunk)

    def to_chunks(a):
        return a.astype(F32).reshape(n, nc, chunk, *a.shape[2:]).swapaxes(0, 1)

    def step(s, inp):
        qc, kc, vc = inp
        att = jnp.einsum('nchd,nmhd->nhcm', qc, kc) * decay[None]
        o = jnp.einsum('nhcm,nmhe->nche', att, vc)
        o = o + jnp.einsum('nchd,nhde->nche', qc, s) * xi.T[None, :, :, None]
        s = g_chunk[None, :, None, None] * s + jnp.einsum('nchd,hc,nche->nhde', kc, zeta, vc)
        return s, o

    s_fin, o = lax.scan(step, s0.astype(F32), (to_chunks(q), to_chunks(k), to_chunks(v)))
    return o.swapaxes(0, 1).reshape(n, L, HB, DVB), s_fin


def decoder_layer(x, c, pos, attn_fn, ret_s0, conv_ctx, ffn_ctx, ret_chunk,
                  ada_w, ada_b, norms, w_in, b_f, ret_gn_g, conv_w, conv_b, conv_ln_g, conv_ln_b,
                  w_o, ffn_up, ffn_conv_w, ffn_conv_b, ffn_down):
    n, L = x.shape[0], x.shape[1]
    mod = jnp.dot(jax.nn.silu(c), ada_w) + ada_b
    sh1, sc1, g1, sh2, sc2, g2 = jnp.split(mod[:, None, :], 6, axis=-1)

    h = rmsnorm(x, norms[0]) * (1 + sc1) + sh1
    aq, ak, av, af, bq, bk, bv, bg, ca, cb = split_cols(jnp.dot(h, w_in))
    logf = jax.nn.log_sigmoid((af + b_f).astype(F32))
    k_a = ak.reshape(n, L, HA, DA)
    v_a = av.reshape(n, L, HA, DA)
    o_a = attn_fn(aq.reshape(n, L, HA, DA), k_a, v_a, logf).reshape(n, L, WA)
    q_b = rope(bq.reshape(n, L, HB, DKB), pos)
    k_b = rope(bk.reshape(n, L, HB, DKB), pos) * (DKB ** -0.5)
    r, ret_s = retention(q_b, k_b, bv.reshape(n, L, HB, DVB), ret_s0, ret_chunk)
    r = (_standardize(r).reshape(n, L, WB) * ret_gn_g.astype(F32)).astype(x.dtype)
    o_b = r * jax.nn.silu(bg)
    u = ca * jax.nn.sigmoid(cb)
    cv, conv_state = causal_dwconv(conv_ctx, u, conv_w, conv_b)
    o_c = jax.nn.silu(layernorm(cv, conv_ln_g, conv_ln_b))
    mix = jnp.dot(jnp.concatenate([o_a, o_b, o_c], axis=-1), w_o)
    x = x + g1 * rmsnorm(mix, norms[1])

    h2 = rmsnorm(x, norms[2]) * (1 + sc2) + sh2
    up, ffn_state = causal_dwconv(ffn_ctx, jnp.dot(h2, ffn_up), ffn_conv_w, ffn_conv_b)
    a, b = jnp.split(up, 2, axis=-1)
    f = jnp.dot(jax.nn.silu(a) * b, ffn_down)
    x = x + g2 * rmsnorm(f, norms[3])
    return x, k_a, v_a, logf, ret_s.astype(ret_s0.dtype), conv_state, ffn_state


def setup_inputs(seed: int = 0) -> dict:
    key = jax.random.key(seed)
    ks = jax.random.split(key, 32)
    n_pages = PAST_LEN // PAGE_SIZE
    n_used = DEC_BATCH * n_pages
    n_pool = n_used + (n_used + 3) // 4
    perm = jax.random.permutation(ks[0], n_pool)
    page_table = perm[:n_used].reshape(DEC_BATCH, n_pages).astype(jnp.int32)

    def nrm(k, shape, s):
        return jax.random.normal(k, shape, F32) * s

    return {
        'x_prompt': nrm(ks[1], (BATCH, SEQ, D_MODEL), 1.0),
        'x_sample': nrm(ks[2], (DEC_BATCH, DEC_SEQ, D_MODEL), 1.0),
        'cache_k': nrm(ks[3], (DEPTH, n_pool, PAGE_SIZE, HA, DA), 1.0),
        'cache_v': nrm(ks[4], (DEPTH, n_pool, PAGE_SIZE, HA, DA), 1.0),
        'cache_logf': jax.nn.log_sigmoid(4.0 + nrm(ks[5], (DEPTH, n_pool, PAGE_SIZE, HA), 1.0)),
        'state_ret': nrm(ks[6], (DEPTH, DEC_BATCH, HB, DKB, DVB), 0.5),
        'state_conv': nrm(ks[7], (DEPTH, DEC_BATCH, CONV_W - 1, DC), 1.0),
        'state_ffn_conv': nrm(ks[8], (DEPTH, DEC_BATCH, FFN_CONV_W - 1, 2 * D_FF), 1.0),
        'page_table': page_table,
        'c_prompt': nrm(ks[9], (BATCH, D_MODEL), 1.0),
        'c_sample': nrm(ks[10], (DEC_BATCH, D_MODEL), 1.0),
        'ada_w': nrm(ks[11], (DEPTH, D_MODEL, 6 * D_MODEL), 0.5 * D_MODEL ** -0.5),
        'ada_b': nrm(ks[12], (DEPTH, 6 * D_MODEL), 0.02),
        'norms': 1.0 + nrm(ks[13], (DEPTH, 4, D_MODEL), 0.1),
        'w_in': nrm(ks[14], (DEPTH, D_MODEL, N_IN), D_MODEL ** -0.5),
        'b_f': 4.0 + nrm(ks[15], (DEPTH, HA), 0.5),
        'ret_gn_g': 1.0 + nrm(ks[16], (DEPTH, WB), 0.1),
        'conv_w': nrm(ks[17], (DEPTH, CONV_W, DC), CONV_W ** -0.5),
        'conv_b': nrm(ks[18], (DEPTH, DC), 0.02),
        'conv_ln_g': 1.0 + nrm(ks[19], (DEPTH, DC), 0.1),
        'conv_ln_b': nrm(ks[20], (DEPTH, DC), 0.02),
        'w_o': nrm(ks[21], (DEPTH, D_MODEL, D_MODEL), D_MODEL ** -0.5),
        'ffn_up': nrm(ks[22], (DEPTH, D_MODEL, 2 * D_FF), D_MODEL ** -0.5),
        'ffn_conv_w': nrm(ks[23], (DEPTH, FFN_CONV_W, 2 * D_FF), FFN_CONV_W ** -0.5),
        'ffn_conv_b': nrm(ks[24], (DEPTH, 2 * D_FF), 0.02),
        'ffn_down': nrm(ks[25], (DEPTH, D_FF, D_MODEL), D_FF ** -0.5),
    }


def reference(x_prompt, x_sample, cache_k, cache_v, cache_logf, state_ret, state_conv, state_ffn_conv,
              page_table, c_prompt, c_sample, ada_w, ada_b, norms, w_in, b_f, ret_gn_g, conv_w, conv_b,
              conv_ln_g, conv_ln_b, w_o, ffn_up, ffn_conv_w, ffn_conv_b, ffn_down):
    bp, sp_len = x_prompt.shape[0], x_prompt.shape[1]
    pos_p = jnp.arange(sp_len)
    pos_s = page_table.shape[1] * PAGE_SIZE + jnp.arange(x_sample.shape[1])
    xp, xs = x_prompt, x_sample
    p_out, s_out = [], []
    for l in range(DEPTH):
        wl = (ada_w[l], ada_b[l], norms[l], w_in[l], b_f[l], ret_gn_g[l], conv_w[l], conv_b[l],
              conv_ln_g[l], conv_ln_b[l], w_o[l], ffn_up[l], ffn_conv_w[l], ffn_conv_b[l], ffn_down[l])
        ret0 = jnp.zeros((bp, HB, DKB, DVB), F32)
        conv0 = jnp.zeros((bp, CONV_W - 1, DC), xp.dtype)
        ffn0 = jnp.zeros((bp, FFN_CONV_W - 1, 2 * D_FF), xp.dtype)
        xp, *st_p = decoder_layer(xp, c_prompt, pos_p, fox_prompt, ret0, conv0, ffn0, RET_CHUNK, *wl)
        p_out.append(st_p)
        attn_s = (lambda q, k, v, lf, l=l: fox_sample(q, k, v, lf, cache_k[l], cache_v[l], cache_logf[l], page_table))
        xs, *st_s = decoder_layer(xs, c_sample, pos_s, attn_s, state_ret[l], state_conv[l], state_ffn_conv[l],
                                  xs.shape[1], *wl)
        s_out.append(st_s)
    pk, pv, plf, pret, pconv, pffn = [jnp.stack(a) for a in zip(*p_out)]
    sk, sv, slf, sret, sconv, sffn = [jnp.stack(a) for a in zip(*s_out)]
    return (xp, xs, pk, pv, plf, pret, pconv, pffn, sk, sv, slf, sret, sconv, sffn)
```

```python
import functools
import math

import jax
import jax.numpy as jnp
import numpy as np
from jax import lax
from jax.experimental import pallas as pl
from jax.experimental.pallas import tpu as pltpu

F32 = jnp.float32
BF16 = jnp.bfloat16

D_MODEL = 1024
HA, DA = 4, 128
WA = HA * DA
HB, DKB, DVB = 4, 64, 64
WB = HB * DVB
DC = 256
CONV_W = 31
D_FF = 2816
PAGE_SIZE = 128
ROPE_BASE = 10000.0
EPS = 1e-6
NEG = -1e30

LANES = 128
SUBLANES = 8
VMEM_LIMIT = 56 * 1024 * 1024

_OQ, _OK, _OV = 0, 512, 1024
_OBQ, _OBK, _OBV, _OBG = 1536, 1792, 2048, 2304
_OCA, _OCB, _OAF = 2560, 2816, 3072
N_IN_PAD = 3200

_LOG_G = [math.log1p(-(2.0 ** (-5.0 - h))) for h in range(HB)]


def _sigmoid(x):
    return 1.0 / (1.0 + jnp.exp(-x))


def _silu(x):
    return x * _sigmoid(x)


def _log_sigmoid(x):
    return jnp.minimum(x, 0.0) - jnp.log1p(jnp.exp(-jnp.abs(x)))


def _rms(x):
    return x * lax.rsqrt(jnp.mean(x * x, axis=-1, keepdims=True) + EPS)


def _dot(a, b):
    return jnp.dot(a, b, preferred_element_type=F32)


def _dot_nt(a, b):
    return lax.dot_general(a, b, (((1,), (1,)), ((), ())), preferred_element_type=F32)


def _params(sem):
    return pltpu.CompilerParams(dimension_semantics=sem, vmem_limit_bytes=VMEM_LIMIT)


def _const_spec(shape):
    nd = len(shape)
    return pl.BlockSpec(shape, lambda *_: (0,) * nd, pipeline_mode=pl.Buffered(1))


def _ada_kernel(c_ref, w_ref, b_ref, o_ref):
    s = _silu(c_ref[...]).astype(BF16)
    o_ref[0] = _dot(s, w_ref[0].astype(BF16)) + b_ref[0]


def _ada_mod(c_all, ada_w, ada_b):
    depth, d, n6 = ada_w.shape
    r = c_all.shape[0]
    tn = 1536
    return pl.pallas_call(
        _ada_kernel,
        grid=(depth, n6 // tn),
        in_specs=[pl.BlockSpec((r, d), lambda l, j: (0, 0)),
                  pl.BlockSpec((1, d, tn), lambda l, j: (l, 0, j)),
                  pl.BlockSpec((1, 1, tn), lambda l, j: (l, 0, j))],
        out_specs=pl.BlockSpec((1, r, tn), lambda l, j: (l, 0, j)),
        out_shape=jax.ShapeDtypeStruct((depth, r, n6), F32),
        compiler_params=_params(("arbitrary", "arbitrary")),
        name="ada_mod",
    )(c_all, ada_w, ada_b.reshape(depth, 1, n6))


def _rope_kernel(inv_ref, cos_ref, sin_ref, *, pos0, stride, tm):
    i = pl.program_id(0)
    row = lax.broadcasted_iota(jnp.int32, (tm, LANES), 0) + i * tm
    lane = lax.broadcasted_iota(jnp.int32, (tm, LANES), 1)
    ang = (pos0 + stride * row).astype(F32) * inv_ref[...]
    cos_ref[...] = jnp.cos(ang)
    s = jnp.sin(ang)
    sin_ref[...] = jnp.where((lane % DKB) < DKB // 2, -s, s)


def _rope_tables(length, pos0, stride):
    half = DKB // 2
    inv = ROPE_BASE ** (-jnp.arange(half, dtype=F32) / half)
    inv = jnp.tile(inv, LANES // half).reshape(1, LANES)
    tm = min(length, 1024)
    out = jax.ShapeDtypeStruct((length, LANES), F32)
    return pl.pallas_call(
        functools.partial(_rope_kernel, pos0=pos0, stride=stride, tm=tm),
        grid=(length // tm,),
        in_specs=[pl.BlockSpec((1, LANES), lambda i: (0, 0))],
        out_specs=[pl.BlockSpec((tm, LANES), lambda i: (i, 0))] * 2,
        out_shape=[out, out],
        compiler_params=_params(("arbitrary",)),
        name="rope_tables",
    )(inv)


def _in_kernel(x_ref, sh_ref, sc_ref, g_ref, w_ref, bf_ref, cos_ref, sin_ref,
               ko_ref, vo_ref, qa_ref, ka_ref, va_ref, lft_ref,
               rq_ref, rk_ref, rv_ref, gate_ref, u_ref):
    x = x_ref[0]
    h = _rms(x) * g_ref[...]
    hb = (h * (1.0 + sc_ref[0]) + sh_ref[0]).astype(BF16)

    def proj(off, width):
        return _dot(hb, w_ref[:, off:off + width])

    qa_ref[0] = (proj(_OQ, WA) * (DA ** -0.5)).astype(BF16)
    k = proj(_OK, WA)
    ko_ref[0] = k
    ka_ref[0] = k.astype(BF16)
    v = proj(_OV, WA)
    vo_ref[0] = v
    va_ref[0] = v.astype(BF16)

    lf = _log_sigmoid(proj(_OAF, LANES) + bf_ref[...])
    lft_ref[0] = lf.T[0:SUBLANES, :]

    cos = cos_ref[...]
    sin = sin_ref[...]
    lane = lax.broadcasted_iota(jnp.int32, cos.shape, 1)
    first = (lane % DKB) < DKB // 2

    def rope(t):
        partner = jnp.where(first, pltpu.roll(t, LANES - DKB // 2, 1), pltpu.roll(t, DKB // 2, 1))
        return t * cos + partner * sin

    bq = proj(_OBQ, WB)
    bk = proj(_OBK, WB)
    for j in range(WB // LANES):
        sl = slice(j * LANES, (j + 1) * LANES)
        rq_ref[0, :, sl] = rope(bq[:, sl]).astype(BF16)
        rk_ref[0, :, sl] = (rope(bk[:, sl]) * (DKB ** -0.5)).astype(BF16)
    rv_ref[0] = proj(_OBV, WB).astype(BF16)
    gate_ref[0] = _silu(proj(_OBG, WB)).astype(BF16)
    u_ref[0] = proj(_OCA, DC) * _sigmoid(proj(_OCB, DC))


def _mod_spec(mod, tm, col):
    rows = mod.shape[1]
    if rows == 1:
        return pl.BlockSpec((1, 1, D_MODEL), lambda b, i: (b, 0, col))
    return pl.BlockSpec((1, tm, D_MODEL), lambda b, i: (b, i, col))


def _in_proj(x, mod, norm_g, w_re, bf_pad, cos, sin, tm):
    bsz, length, d = x.shape
    grid = (bsz, length // tm)

    def rows(width):
        return pl.BlockSpec((1, tm, width), lambda b, i: (b, i, 0))

    def shp(width, dt):
        return jax.ShapeDtypeStruct((bsz, length, width), dt)

    tab = pl.BlockSpec((tm, LANES), lambda b, i: (i, 0))
    return pl.pallas_call(
        _in_kernel,
        grid=grid,
        in_specs=[rows(d), _mod_spec(mod, tm, 0), _mod_spec(mod, tm, 1),
                  _const_spec((1, d)), _const_spec((d, N_IN_PAD)), _const_spec((1, LANES)), tab, tab],
        out_specs=[rows(WA), rows(WA), rows(WA), rows(WA), rows(WA),
                   pl.BlockSpec((1, SUBLANES, tm), lambda b, i: (b, 0, i)),
                   rows(WB), rows(WB), rows(WB), rows(WB), rows(DC)],
        out_shape=[shp(WA, F32), shp(WA, F32), shp(WA, BF16), shp(WA, BF16), shp(WA, BF16),
                   jax.ShapeDtypeStruct((bsz, SUBLANES, length), F32),
                   shp(WB, BF16), shp(WB, BF16), shp(WB, BF16), shp(WB, BF16), shp(DC, F32)],
        compiler_params=_params(("arbitrary", "arbitrary")),
        name="in_proj",
    )(x, mod, mod, norm_g, w_re, bf_pad, cos, sin)


def _cumsum_kernel(x_ref, o_ref):
    x = x_ref[0]
    length = x.shape[1]
    lane = lax.broadcasted_iota(jnp.int32, x.shape, 1)
    s = 1
    while s < length:
        x = x + jnp.where(lane >= s, pltpu.roll(x, s, 1), 0.0)
        s *= 2
    o_ref[0] = x


def _cumsum(lft):
    bsz, r, length = lft.shape
    return pl.pallas_call(
        _cumsum_kernel,
        grid=(bsz,),
        in_specs=[pl.BlockSpec((1, r, length), lambda b: (b, 0, 0))],
        out_specs=pl.BlockSpec((1, r, length), lambda b: (b, 0, 0)),
        out_shape=jax.ShapeDtypeStruct(lft.shape, F32),
        compiler_params=_params(("arbitrary",)),
        name="logf_cumsum",
    )(lft)


def _fox_kernel(q_ref, k_ref, v_ref, ck_ref, o_ref, *, tq):
    qi = pl.program_id(1)
    row = lax.broadcasted_iota(jnp.int32, (tq, tq), 0)
    col = lax.broadcasted_iota(jnp.int32, (tq, tq), 1)
    causal = row >= col
    for h in range(HA):
        sl = slice(h * DA, (h + 1) * DA)
        q = q_ref[0, :, sl]

        def step(kb, carry, masked, q=q, sl=sl, h=h):
            m, l, acc = carry
            off = pl.multiple_of(kb * tq, tq)
            k = k_ref[0, pl.ds(off, tq), sl]
            v = v_ref[0, pl.ds(off, tq), sl]
            s = _dot_nt(q, k) - ck_ref[0, kb, h:h + 1, :]
            if masked:
                s = jnp.where(causal, s, NEG)
            m_new = jnp.maximum(m, jnp.max(s, axis=-1, keepdims=True))
            p = jnp.exp(s - m_new)
            alpha = jnp.exp(m - m_new)
            l = alpha * l + jnp.sum(p, axis=-1, keepdims=True)
            acc = alpha * acc + _dot(p.astype(BF16), v)
            return m_new, l, acc

        init = (jnp.full((tq, 1), NEG, F32), jnp.zeros((tq, 1), F32), jnp.zeros((tq, DA), F32))
        carry = lax.fori_loop(0, qi, functools.partial(step, masked=False), init)
        m, l, acc = step(qi, carry, True)
        o_ref[0, :, sl] = (acc / l).astype(BF16)


def _fox_prompt(qa, ka, va, cum, tq):
    bsz, length, _ = qa.shape
    nb = length // tq
    ck = cum.reshape(bsz, SUBLANES, nb, tq).transpose(0, 2, 1, 3)
    full = pl.BlockSpec((1, length, WA), lambda b, i: (b, 0, 0))
    return pl.pallas_call(
        functools.partial(_fox_kernel, tq=tq),
        grid=(bsz, nb),
        in_specs=[pl.BlockSpec((1, tq, WA), lambda b, i: (b, i, 0)), full, full,
                  pl.BlockSpec((1, nb, SUBLANES, tq), lambda b, i: (b, 0, 0, 0))],
        out_specs=pl.BlockSpec((1, tq, WA), lambda b, i: (b, i, 0)),
        out_shape=jax.ShapeDtypeStruct((bsz, length, WA), BF16),
        compiler_params=_params(("arbitrary", "arbitrary")),
        name="fox_prompt",
    )(qa, ka, va, ck)


def _head_mean(z, avg):
    hi = z.astype(BF16)
    lo = (z - hi.astype(F32)).astype(BF16)
    return _dot(hi, avg) + _dot(lo, avg)


def _group_avg_matrix():
    r = lax.broadcasted_iota(jnp.int32, (WB, WB), 0) // DVB
    c = lax.broadcasted_iota(jnp.int32, (WB, WB), 1) // DVB
    return r == c


def _lane_log_g(shape, axis):
    head = lax.broadcasted_iota(jnp.int32, shape, axis) // DVB
    lg = jnp.full(shape, _LOG_G[HB - 1], F32)
    for h in range(HB - 1):
        lg = jnp.where(head == h, _LOG_G[h], lg)
    return lg


def _ret_kernel(q_ref, k_ref, v_ref, gate_ref, gn_ref, o_ref, so_ref, s_ref, *, chunk):
    ci = pl.program_id(1)

    @pl.when(ci == 0)
    def _():
        s_ref[...] = jnp.zeros_like(s_ref)

    q = q_ref[0]
    k = k_ref[0]
    v = v_ref[0]
    head = lax.broadcasted_iota(jnp.int32, (1, WB), 1) // DVB
    r_i = lax.broadcasted_iota(jnp.int32, (chunk, chunk), 0)
    c_i = lax.broadcasted_iota(jnp.int32, (chunk, chunk), 1)
    rel = (r_i - c_i).astype(F32)
    o = jnp.zeros((chunk, WB), F32)
    for h in range(HB):
        decay = jnp.where(rel >= 0, jnp.exp(_LOG_G[h] * jnp.maximum(rel, 0.0)), 0.0)
        mh = head == h
        att = _dot_nt(jnp.where(mh, q, jnp.zeros_like(q)), k) * decay
        o = o + jnp.where(mh, _dot(att.astype(BF16), v), 0.0)

    lg = _lane_log_g((chunk, WB), 1)
    idx = lax.broadcasted_iota(jnp.int32, (chunk, WB), 0).astype(F32)
    xi = jnp.exp(lg * (idx + 1.0))
    zeta = jnp.exp(lg * (chunk - 1.0 - idx))
    s_old = s_ref[...]
    o = o + _dot(q, s_old.astype(BF16)) * xi
    kz_t = (k.astype(F32) * zeta).T.astype(BF16)
    same_head = _group_avg_matrix()
    g_chunk = jnp.exp(_lane_log_g((WB, WB), 0) * float(chunk))
    s_ref[...] = jnp.where(same_head, g_chunk * s_old + _dot(kz_t, v), 0.0)

    avg = jnp.where(same_head, 1.0 / DVB, 0.0).astype(BF16)
    xc = o - _head_mean(o, avg)
    var = _head_mean(xc * xc, avg)
    r = xc * lax.rsqrt(var + EPS) * gn_ref[...]
    o_ref[0] = (r * gate_ref[0].astype(F32)).astype(BF16)

    @pl.when(ci == pl.num_programs(1) - 1)
    def _():
        so_ref[0] = s_ref[...]


def _ret_prompt(rq, rk, rv, gate, gn, chunk):
    bsz, length, _ = rq.shape
    rows = pl.BlockSpec((1, chunk, WB), lambda b, i: (b, i, 0))
    return pl.pallas_call(
        functools.partial(_ret_kernel, chunk=chunk),
        grid=(bsz, length // chunk),
        in_specs=[rows, rows, rows, rows, _const_spec((1, WB))],
        out_specs=[rows, pl.BlockSpec((1, WB, WB), lambda b, i: (b, 0, 0))],
        out_shape=[jax.ShapeDtypeStruct((bsz, length, WB), BF16),
                   jax.ShapeDtypeStruct((bsz, WB, WB), F32)],
        scratch_shapes=[pltpu.VMEM((WB, WB), F32)],
        compiler_params=_params(("arbitrary", "arbitrary")),
        name="ret_prompt",
    )(rq, rk, rv, gate, gn)


_CONV_PAD = 32


def _layernorm(x, g, b):
    mu = jnp.mean(x, axis=-1, keepdims=True)
    xc = x - mu
    return xc * lax.rsqrt(jnp.mean(xc * xc, axis=-1, keepdims=True) + EPS) * g + b


def _conv_kernel(u_ref, w_ref, b_ref, lg_ref, lb_ref, o_ref, st_ref, ext_ref, *, tm):
    i = pl.program_id(1)

    @pl.when(i == 0)
    def _():
        ext_ref[0:_CONV_PAD, :] = jnp.zeros((_CONV_PAD, DC), F32)

    ext_ref[_CONV_PAD:_CONV_PAD + tm, :] = u_ref[0]
    acc = jnp.zeros((tm, DC), F32) + b_ref[...]
    base = _CONV_PAD - (CONV_W - 1)
    for j in range(CONV_W):
        acc = acc + w_ref[j:j + 1, :] * ext_ref[base + j:base + j + tm, :]
    o_ref[0] = _silu(_layernorm(acc, lg_ref[...], lb_ref[...])).astype(BF16)

    @pl.when(i == pl.num_programs(1) - 1)
    def _():
        st_ref[0] = ext_ref[tm + base:tm + _CONV_PAD, :]

    ext_ref[0:_CONV_PAD, :] = ext_ref[tm:tm + _CONV_PAD, :]


def _conv_prompt(u, w, b, lg, lb, tm):
    bsz, length, _ = u.shape
    rows = pl.BlockSpec((1, tm, DC), lambda bb, i: (bb, i, 0))
    return pl.pallas_call(
        functools.partial(_conv_kernel, tm=tm),
        grid=(bsz, length // tm),
        in_specs=[rows, _const_spec((CONV_W, DC)), _const_spec((1, DC)),
                  _const_spec((1, DC)), _const_spec((1, DC))],
        out_specs=[rows, pl.BlockSpec((1, CONV_W - 1, DC), lambda bb, i: (bb, 0, 0))],
        out_shape=[jax.ShapeDtypeStruct((bsz, length, DC), BF16),
                   jax.ShapeDtypeStruct((bsz, CONV_W - 1, DC), F32)],
        scratch_shapes=[pltpu.VMEM((tm + _CONV_PAD, DC), F32)],
        compiler_params=_params(("arbitrary", "arbitrary")),
        name="conv_prompt",
    )(u, w, b, lg, lb)


_FF_CHUNK = 256


def _ffn_kernel(*refs, tm, step_mode):
    if step_mode:
        (x_ref, oa_ref, ob_ref, oc_ref, g1_ref, sh2_ref, sc2_ref, g2_ref, n_ref,
         wo_ref, wup_ref, cw_ref, cb_ref, wdn_ref, st_ref, y_ref, ns_ref) = refs
    else:
        (x_ref, oa_ref, ob_ref, oc_ref, g1_ref, sh2_ref, sc2_ref, g2_ref, n_ref,
         wo_ref, wup_ref, cw_ref, cb_ref, wdn_ref, y_ref, ns_ref, ctx_ref, buf_ref) = refs
        i = pl.program_id(1)

        @pl.when(i == 0)
        def _():
            ctx_ref[...] = jnp.zeros_like(ctx_ref)

    mix = (_dot(oa_ref[0], wo_ref[0:WA, :]) + _dot(ob_ref[0], wo_ref[WA:WA + WB, :])
           + _dot(oc_ref[0], wo_ref[WA + WB:, :]))
    x1 = x_ref[0] + g1_ref[0] * (_rms(mix) * n_ref[1:2, :])
    h2 = ((_rms(x1) * n_ref[2:3, :]) * (1.0 + sc2_ref[0]) + sh2_ref[0]).astype(BF16)

    acc = jnp.zeros((tm, D_MODEL), F32)
    for c in range(D_FF // _FF_CHUNK):
        halves = []
        for half in range(2):
            c0 = half * D_FF + c * _FF_CHUNK
            cs = slice(c0, c0 + _FF_CHUNK)
            up = _dot(h2, wup_ref[:, cs])
            if step_mode:
                prev2 = st_ref[0, :, cs]
                prev1 = st_ref[0, :, 2 * D_FF + c0:2 * D_FF + c0 + _FF_CHUNK]
                ns_ref[0, :, cs] = prev1
                ns_ref[0, :, 2 * D_FF + c0:2 * D_FF + c0 + _FF_CHUNK] = up
            else:
                buf_ref[0:SUBLANES, :] = ctx_ref[:, cs]
                buf_ref[SUBLANES:SUBLANES + tm, :] = up
                prev1 = buf_ref[SUBLANES - 1:SUBLANES - 1 + tm, :]
                prev2 = buf_ref[SUBLANES - 2:SUBLANES - 2 + tm, :]
                ctx_ref[:, cs] = buf_ref[tm:tm + SUBLANES, :]
            halves.append(cw_ref[0:1, cs] * prev2 + cw_ref[1:2, cs] * prev1
                          + cw_ref[2:3, cs] * up + cb_ref[:, cs])
        act = (_silu(halves[0]) * halves[1]).astype(BF16)
        acc = acc + _dot(act, wdn_ref[c * _FF_CHUNK:(c + 1) * _FF_CHUNK, :])
    y_ref[0] = x1 + g2_ref[0] * (_rms(acc) * n_ref[3:4, :])

    if not step_mode:
        @pl.when(i == pl.num_programs(1) - 1)
        def _():
            ns_ref[0] = ctx_ref[SUBLANES - 2:SUBLANES, :]


def _out_ffn(x, oa, ob, oc, mod, norms, wo, wup, cw, cb, wdn, tm, state=None, layer=0):
    bsz, length, d = x.shape
    step_mode = state is not None

    def rows(width):
        return pl.BlockSpec((1, tm, width), lambda b, i: (b, i, 0))

    in_specs = [rows(d), rows(WA), rows(WB), rows(DC),
                _mod_spec(mod, tm, 2), _mod_spec(mod, tm, 3), _mod_spec(mod, tm, 4), _mod_spec(mod, tm, 5),
                _const_spec((4, d)), _const_spec((d, d)), _const_spec((d, 2 * D_FF)),
                _const_spec((3, 2 * D_FF)), _const_spec((1, 2 * D_FF)), _const_spec((D_FF, d))]
    args = [x, oa, ob, oc, mod, mod, mod, mod, norms, wo, wup, cw, cb, wdn]
    if step_mode:
        in_specs.append(pl.BlockSpec((1, tm, 4 * D_FF), lambda b, i: (layer, i, 0)))
        args.append(state)
        ns_spec = pl.BlockSpec((1, tm, 4 * D_FF), lambda b, i: (b, i, 0))
        ns_shape = jax.ShapeDtypeStruct((bsz, length, 4 * D_FF), F32)
        scratch = []
    else:
        ns_spec = pl.BlockSpec((1, 2, 2 * D_FF), lambda b, i: (b, 0, 0))
        ns_shape = jax.ShapeDtypeStruct((bsz, 2, 2 * D_FF), F32)
        scratch = [pltpu.VMEM((SUBLANES, 2 * D_FF), F32), pltpu.VMEM((tm + SUBLANES, _FF_CHUNK), F32)]
    return pl.pallas_call(
        functools.partial(_ffn_kernel, tm=tm, step_mode=step_mode),
        grid=(bsz, length // tm),
        in_specs=in_specs,
        out_specs=[rows(d), ns_spec],
        out_shape=[jax.ShapeDtypeStruct((bsz, length, d), F32), ns_shape],
        scratch_shapes=scratch,
        compiler_params=_params(("arbitrary", "arbitrary")),
        name="out_ffn_step" if step_mode else "out_ffn_seq",
    )(*args)


def _paged_kernel(pt_ref, q_ref, kn_ref, vn_ref, lfn_ref, *rest, n_pages):
    kp = rest[0:n_pages]
    vp = rest[n_pages:2 * n_pages]
    lp = rest[2 * n_pages:3 * n_pages]
    o_ref = rest[3 * n_pages]
    past = n_pages * PAGE_SIZE

    rowh = lax.broadcasted_iota(jnp.int32, (SUBLANES, WA), 0)
    laneh = lax.broadcasted_iota(jnp.int32, (SUBLANES, WA), 1) // DA
    diag = rowh == laneh
    qbd = jnp.where(diag, q_ref[0], 0.0)
    qbd_b = qbd.astype(BF16)

    lf = jnp.concatenate([r[0, 0] for r in lp], axis=1)
    lane = lax.broadcasted_iota(jnp.int32, lf.shape, 1)
    suf = lf
    s = 1
    while s < past:
        suf = suf + jnp.where(lane + s < past, pltpu.roll(suf, past - s, 1), 0.0)
        s *= 2
    bias = suf - lf + lfn_ref[0]

    logits = jnp.concatenate([_dot_nt(qbd_b, r[0, 0].astype(BF16)) for r in kp], axis=1) + bias
    s_new = jnp.sum(qbd * kn_ref[0], axis=-1, keepdims=True)
    m = jnp.maximum(jnp.max(logits, axis=-1, keepdims=True), s_new)
    p = jnp.exp(logits - m)
    p_new = jnp.exp(s_new - m)
    l = jnp.sum(p, axis=-1, keepdims=True) + p_new
    pb = p.astype(BF16)
    acc = p_new * vn_ref[0]
    for j in range(n_pages):
        acc = acc + _dot(pb[:, j * PAGE_SIZE:(j + 1) * PAGE_SIZE], vp[j][0, 0].astype(BF16))
    o_ref[0] = jnp.sum(jnp.where(diag, acc / l, 0.0), axis=0, keepdims=True).astype(BF16)


def _fox_sample(page_table, q, k_new, v_new, lf_new, pool_k, pool_v, pool_lf, layer):
    n, n_pages = page_table.shape
    pt = page_table.reshape(-1)
    row = pl.BlockSpec((1, 1, WA), lambda i, pt: (i, 0, 0))

    def page(width_shape, j):
        return pl.BlockSpec((1, 1) + width_shape, lambda i, pt, j=j: (layer, pt[i * n_pages + j], 0, 0))

    in_specs = ([row, row, row, pl.BlockSpec((1, SUBLANES, 1), lambda i, pt: (i, 0, 0))]
                + [page((PAGE_SIZE, WA), j) for j in range(n_pages)]
                + [page((PAGE_SIZE, WA), j) for j in range(n_pages)]
                + [page((SUBLANES, PAGE_SIZE), j) for j in range(n_pages)])
    grid_spec = pltpu.PrefetchScalarGridSpec(
        num_scalar_prefetch=1, grid=(n,), in_specs=in_specs,
        out_specs=pl.BlockSpec((1, 1, WA), lambda i, pt: (i, 0, 0)))
    return pl.pallas_call(
        functools.partial(_paged_kernel, n_pages=n_pages),
        grid_spec=grid_spec,
        out_shape=jax.ShapeDtypeStruct((n, 1, WA), BF16),
        compiler_params=_params(("arbitrary",)),
        name="fox_sample",
    )(pt, q, k_new, v_new, lf_new, *([pool_k] * n_pages), *([pool_v] * n_pages), *([pool_lf] * n_pages))


def _sret_kernel(q_ref, k_ref, v_ref, gate_ref, gn_ref, s0_ref, o_ref, sn_ref):
    q = q_ref[0]
    k = k_ref[0]
    v = v_ref[0]
    hw = DKB * DVB
    lane = lax.broadcasted_iota(jnp.int32, (DKB, hw), 1)
    rowi = lax.broadcasted_iota(jnp.int32, (DKB, hw), 0)
    rep = (lane // DVB == rowi).astype(BF16)
    til = (lane % DVB == rowi).astype(BF16)
    outs = []
    for h in range(HB):
        sl = slice(h * DKB, (h + 1) * DKB)
        g = math.exp(_LOG_G[h])
        sn = g * s0_ref[0, :, h * hw:(h + 1) * hw] + _dot(k[:, sl], rep) * _dot(v[:, sl], til)
        sn_ref[0, :, h * hw:(h + 1) * hw] = sn
        prod = _dot(q[:, sl], rep) * sn
        acc = prod[:, 0:LANES]
        for j in range(1, hw // LANES):
            acc = acc + prod[:, j * LANES:(j + 1) * LANES]
        oh = acc[:, 0:DVB] + acc[:, DVB:LANES]
        mu = jnp.mean(oh, axis=-1, keepdims=True)
        xc = oh - mu
        outs.append(xc * lax.rsqrt(jnp.mean(xc * xc, axis=-1, keepdims=True) + EPS))
    r = jnp.concatenate(outs, axis=1) * gn_ref[...]
    o_ref[0] = (r * gate_ref[0].astype(F32)).astype(BF16)


def _ret_sample(rq, rk, rv, gate, gn, state, layer, nb):
    _, n, _ = rq.shape
    sw = HB * DKB * DVB
    rows = pl.BlockSpec((1, nb, WB), lambda i: (0, i, 0))
    return pl.pallas_call(
        _sret_kernel,
        grid=(n // nb,),
        in_specs=[rows, rows, rows, rows, _const_spec((1, WB)),
                  pl.BlockSpec((1, nb, sw), lambda i: (layer, i, 0))],
        out_specs=[rows, pl.BlockSpec((1, nb, sw), lambda i: (0, i, 0))],
        out_shape=[jax.ShapeDtypeStruct((1, n, WB), BF16), jax.ShapeDtypeStruct((1, n, sw), F32)],
        compiler_params=_params(("arbitrary",)),
        name="ret_sample",
    )(rq, rk, rv, gate, gn, state)


def _sconv_kernel(ctx_ref, u_ref, w_ref, b_ref, lg_ref, lb_ref, o_ref, ns_ref):
    ctx = ctx_ref[0]
    u = u_ref[...]
    w = w_ref[...]
    cv = (jnp.sum(ctx * w[0:CONV_W - 1, :][None], axis=1, keepdims=True)
          + u * w[CONV_W - 1:CONV_W, :][None] + b_ref[...][None])
    o_ref[...] = _silu(_layernorm(cv, lg_ref[...][None], lb_ref[...][None])).astype(BF16)
    ns_ref[0, :, 0:CONV_W - 2, :] = ctx[:, 1:CONV_W - 1, :]
    ns_ref[0, :, CONV_W - 2:CONV_W - 1, :] = u


def _conv_sample(state, u, w, b, lg, lb, layer, nb):
    n = u.shape[0]
    return pl.pallas_call(
        _sconv_kernel,
        grid=(n // nb,),
        in_specs=[pl.BlockSpec((1, nb, CONV_W - 1, DC), lambda i: (layer, i, 0, 0)),
                  pl.BlockSpec((nb, 1, DC), lambda i: (i, 0, 0)),
                  _const_spec((CONV_W, DC)), _const_spec((1, DC)), _const_spec((1, DC)), _const_spec((1, DC))],
        out_specs=[pl.BlockSpec((nb, 1, DC), lambda i: (i, 0, 0)),
                   pl.BlockSpec((1, nb, CONV_W - 1, DC), lambda i: (0, i, 0, 0))],
        out_shape=[jax.ShapeDtypeStruct((n, 1, DC), BF16),
                   jax.ShapeDtypeStruct((1, n, CONV_W - 1, DC), F32)],
        compiler_params=_params(("arbitrary",)),
        name="conv_sample",
    )(state, u, w, b, lg, lb)


def _reorder_w_in(w):
    o = 3 * WA
    main = jnp.concatenate([w[:, :o], w[:, o + HA:]], axis=1)
    tail = jnp.pad(w[:, o:o + HA], ((0, 0), (0, LANES - HA)))
    return jnp.concatenate([main, tail], axis=1).astype(BF16)


def _pick(total, pref):
    t = min(total, pref)
    assert total % t == 0
    return t


def kernel(x_prompt, x_sample, cache_k, cache_v, cache_logf, state_ret, state_conv, state_ffn_conv, page_table,
           c_prompt, c_sample, ada_w, ada_b, norms, w_in, b_f, ret_gn_g, conv_w, conv_b, conv_ln_g, conv_ln_b,
           w_o, ffn_up, ffn_conv_w, ffn_conv_b, ffn_down):
    depth = ada_w.shape[0]
    bp, sp, d = x_prompt.shape
    ns, ts, _ = x_sample.shape
    assert ts == 1 and d == D_MODEL
    n_pool = cache_k.shape[1]
    n_pages = page_table.shape[1]

    n_c = ns + bp
    n_c_pad = -(-n_c // SUBLANES) * SUBLANES
    c_all = jnp.pad(jnp.concatenate([c_sample, c_prompt], axis=0), ((0, n_c_pad - n_c), (0, 0)))
    mod_all = _ada_mod(c_all, ada_w, ada_b)

    cos_p, sin_p = _rope_tables(sp, 0, 1)
    cos_s, sin_s = _rope_tables(ns, n_pages * PAGE_SIZE, 0)

    pool_k = cache_k.reshape(depth, n_pool, PAGE_SIZE, WA)
    pool_v = cache_v.reshape(depth, n_pool, PAGE_SIZE, WA)
    pool_lf = jnp.pad(cache_logf.transpose(0, 1, 3, 2), ((0, 0), (0, 0), (0, SUBLANES - HA), (0, 0)))
    st_ret = state_ret.reshape(depth, ns, HB * DKB * DVB)
    st_ffn = state_ffn_conv.reshape(depth, ns, 4 * D_FF)

    tm_p = _pick(sp, 512)
    tq = _pick(sp, 512)
    chunk = _pick(sp, 128)
    nb_s = _pick(ns, 32)

    xp = x_prompt
    xs = x_sample.reshape(1, ns, d)
    p_out, s_out = [], []
    for l in range(depth):
        w_re = _reorder_w_in(w_in[l])
        bf_pad = jnp.pad(b_f[l], (0, LANES - HA)).reshape(1, LANES)
        wo = w_o[l].astype(BF16)
        wup = ffn_up[l].astype(BF16)
        wdn = ffn_down[l].astype(BF16)
        n0 = norms[l, 0:1]
        gn = ret_gn_g[l].reshape(1, WB)
        cb = conv_b[l].reshape(1, DC)
        clg = conv_ln_g[l].reshape(1, DC)
        clb = conv_ln_b[l].reshape(1, DC)
        fcb = ffn_conv_b[l].reshape(1, 2 * D_FF)
        mod_s = mod_all[l, 0:ns].reshape(1, ns, 6 * d)
        mod_p = mod_all[l, ns:ns + bp].reshape(bp, 1, 6 * d)

        ko, vo, qa, ka, va, lft, rq, rk, rv, gate, u = _in_proj(xp, mod_p, n0, w_re, bf_pad, cos_p, sin_p, tm_p)
        cum = _cumsum(lft)
        oa = _fox_prompt(qa, ka, va, cum, tq)
        ob, s_bd = _ret_prompt(rq, rk, rv, gate, gn, chunk)
        oc, conv_st = _conv_prompt(u, conv_w[l], cb, clg, clb, tm_p)
        xp, ffn_st = _out_ffn(xp, oa, ob, oc, mod_p, norms[l], wo, wup, ffn_conv_w[l], fcb, wdn, tm_p)
        s5 = s_bd.reshape(bp, HB, DKB, HB, DVB)
        ret_st = jnp.stack([s5[:, h, :, h, :] for h in range(HB)], axis=1)
        p_out.append((ko.reshape(bp, sp, HA, DA), vo.reshape(bp, sp, HA, DA),
                      lft[:, 0:HA, :].transpose(0, 2, 1), ret_st, conv_st, ffn_st))

        ko, vo, qa, ka, va, lft, rq, rk, rv, gate, u = _in_proj(xs, mod_s, n0, w_re, bf_pad, cos_s, sin_s, ns)
        q3 = qa.astype(F32).reshape(ns, 1, WA)
        lfn = lft[0].T.reshape(ns, SUBLANES, 1)
        oa = _fox_sample(page_table, q3, ko.reshape(ns, 1, WA), vo.reshape(ns, 1, WA), lfn,
                         pool_k, pool_v, pool_lf, l)
        ob, ret_new = _ret_sample(rq, rk, rv, gate, gn, st_ret, l, nb_s)
        oc, conv_new = _conv_sample(state_conv, u.reshape(ns, 1, DC), conv_w[l], cb, clg, clb, l, nb_s)
        xs, ffn_new = _out_ffn(xs, oa.reshape(1, ns, WA), ob, oc.reshape(1, ns, DC), mod_s, norms[l],
                               wo, wup, ffn_conv_w[l], fcb, wdn, ns, state=st_ffn, layer=l)
        s_out.append((ko.reshape(ns, 1, HA, DA), vo.reshape(ns, 1, HA, DA),
                      lft[0, 0:HA, :].T.reshape(ns, 1, HA), ret_new.reshape(ns, HB, DKB, DVB),
                      conv_new[0], ffn_new.reshape(ns, 2, 2 * D_FF)))

    pk, pv, plf, pret, pconv, pffn = [jnp.stack(a) for a in zip(*p_out)]
    sk, sv, slf, sret, sconv, sffn = [jnp.stack(a) for a in zip(*s_out)]
    return (xp, xs.reshape(ns, 1, d), pk, pv, plf, pret, pconv, pffn, sk, sv, slf, sret, sconv, sffn)
```

```python
import functools
import math

import jax
import jax.numpy as jnp
import numpy as np
from jax import lax
from jax.experimental import pallas as pl
from jax.experimental.pallas import tpu as pltpu

F32 = jnp.float32
BF16 = jnp.bfloat16

D_MODEL = 1024
HA, DA = 4, 128
WA = HA * DA
HB, DKB, DVB = 4, 64, 64
WB = HB * DVB
DC = 256
CONV_W = 31
D_FF = 2816
PAGE_SIZE = 128
ROPE_BASE = 10000.0
EPS = 1e-6
NEG = -1e30
LOG2E = math.log2(math.e)

LANES = 128
SUBLANES = 8
VMEM_LIMIT = 56 * 1024 * 1024

_OQ, _OK, _OV = 0, 512, 1024
_OBQ, _OBK, _OBV, _OBG = 1536, 1792, 2048, 2304
_OCA, _OCB, _OAF = 2560, 2816, 3072
N_IN_PAD = 3200

_LOG_G = [math.log1p(-(2.0 ** (-5.0 - h))) for h in range(HB)]


def _sigmoid(x):
    return 1.0 / (1.0 + jnp.exp(-x))


def _silu(x):
    return x * _sigmoid(x)


def _log_sigmoid(x):
    return jnp.minimum(x, 0.0) - jnp.log1p(jnp.exp(-jnp.abs(x)))


def _rms(x):
    return x * lax.rsqrt(jnp.mean(x * x, axis=-1, keepdims=True) + EPS)


def _dot(a, b):
    return jnp.dot(a, b, preferred_element_type=F32)


def _dot_nt(a, b):
    return lax.dot_general(a, b, (((1,), (1,)), ((), ())), preferred_element_type=F32)


def _params(sem):
    return pltpu.CompilerParams(dimension_semantics=sem, vmem_limit_bytes=VMEM_LIMIT)


def _const_spec(shape):
    nd = len(shape)
    return pl.BlockSpec(shape, lambda *_: (0,) * nd, pipeline_mode=pl.Buffered(1))


def _ada_kernel(c_ref, w_ref, b_ref, o_ref):
    s = _silu(c_ref[...]).astype(BF16)
    o_ref[0] = _dot(s, w_ref[0].astype(BF16)) + b_ref[0]


def _ada_mod(c_all, ada_w, ada_b):
    depth, d, n6 = ada_w.shape
    r = c_all.shape[0]
    tn = 1536
    return pl.pallas_call(
        _ada_kernel,
        grid=(depth, n6 // tn),
        in_specs=[pl.BlockSpec((r, d), lambda l, j: (0, 0)),
                  pl.BlockSpec((1, d, tn), lambda l, j: (l, 0, j)),
                  pl.BlockSpec((1, 1, tn), lambda l, j: (l, 0, j))],
        out_specs=pl.BlockSpec((1, r, tn), lambda l, j: (l, 0, j)),
        out_shape=jax.ShapeDtypeStruct((depth, r, n6), F32),
        compiler_params=_params(("arbitrary", "arbitrary")),
        name="ada_mod",
    )(c_all, ada_w, ada_b.reshape(depth, 1, n6))


def _rope_kernel(inv_ref, cos_ref, sin_ref, *, pos0, stride, tm):
    i = pl.program_id(0)
    row = lax.broadcasted_iota(jnp.int32, (tm, LANES), 0) + i * tm
    lane = lax.broadcasted_iota(jnp.int32, (tm, LANES), 1)
    ang = (pos0 + stride * row).astype(F32) * inv_ref[...]
    cos_ref[...] = jnp.cos(ang)
    s = jnp.sin(ang)
    sin_ref[...] = jnp.where((lane % DKB) < DKB // 2, -s, s)


def _rope_tables(length, pos0, stride):
    half = DKB // 2
    inv = ROPE_BASE ** (-jnp.arange(half, dtype=F32) / half)
    inv = jnp.tile(inv, LANES // half).reshape(1, LANES)
    tm = min(length, 1024)
    out = jax.ShapeDtypeStruct((length, LANES), F32)
    return pl.pallas_call(
        functools.partial(_rope_kernel, pos0=pos0, stride=stride, tm=tm),
        grid=(length // tm,),
        in_specs=[pl.BlockSpec((1, LANES), lambda i: (0, 0))],
        out_specs=[pl.BlockSpec((tm, LANES), lambda i: (i, 0))] * 2,
        out_shape=[out, out],
        compiler_params=_params(("arbitrary",)),
        name="rope_tables",
    )(inv)


def _in_kernel(*refs, tm, aliased):
    if aliased:
        refs = refs[:8] + refs[10:]
    (x_ref, sh_ref, sc_ref, g_ref, w_ref, bf_ref, cos_ref, sin_ref,
     ko_ref, vo_ref, qa_ref, ka_ref, va_ref, lft_ref, rq_ref, rk_ref, rv_ref, gate_ref, u_ref) = refs
    x = x_ref[0]
    h = _rms(x) * g_ref[...]
    hb = (h * (1.0 + sc_ref[0]) + sh_ref[0]).astype(BF16)

    def proj(off, width):
        return _dot(hb, w_ref[:, off:off + width])

    qa_ref[0] = (proj(_OQ, WA) * (DA ** -0.5 * LOG2E)).astype(BF16)
    k = proj(_OK, WA)
    ka_ref[0] = k.astype(BF16)
    v = proj(_OV, WA)
    va_ref[0] = v.astype(BF16)
    for hd in range(HA):
        ko_ref[0, 0, pl.ds(hd, tm, stride=HA), :] = k[:, hd * DA:(hd + 1) * DA]
        vo_ref[0, 0, pl.ds(hd, tm, stride=HA), :] = v[:, hd * DA:(hd + 1) * DA]

    lf = _log_sigmoid(proj(_OAF, LANES) + bf_ref[...])
    lft_ref[0] = lf.T[0:SUBLANES, :]

    cos = cos_ref[...]
    sin = sin_ref[...]
    lane = lax.broadcasted_iota(jnp.int32, cos.shape, 1)
    first = (lane % DKB) < DKB // 2

    def rope(t):
        partner = jnp.where(first, pltpu.roll(t, LANES - DKB // 2, 1), pltpu.roll(t, DKB // 2, 1))
        return t * cos + partner * sin

    bq = proj(_OBQ, WB)
    bk = proj(_OBK, WB)
    for j in range(WB // LANES):
        sl = slice(j * LANES, (j + 1) * LANES)
        rq_ref[0, :, sl] = rope(bq[:, sl]).astype(BF16)
        rk_ref[0, :, sl] = (rope(bk[:, sl]) * (DKB ** -0.5)).astype(BF16)
    rv_ref[0] = proj(_OBV, WB).astype(BF16)
    gate_ref[0] = _silu(proj(_OBG, WB)).astype(BF16)
    u_ref[0] = proj(_OCA, DC) * _sigmoid(proj(_OCB, DC))


def _mod_spec(mod, tm, col):
    rows = mod.shape[1]
    if rows == 1:
        return pl.BlockSpec((1, 1, D_MODEL), lambda b, i: (b, 0, col))
    return pl.BlockSpec((1, tm, D_MODEL), lambda b, i: (b, i, col))


def _in_proj(x, mod, norm_g, w_re, bf_pad, cos, sin, tm, depth, layer, kv_prev):
    bsz, length, d = x.shape
    grid = (bsz, length // tm)

    def rows(width):
        return pl.BlockSpec((1, tm, width), lambda b, i: (b, i, 0))

    def shp(width, dt):
        return jax.ShapeDtypeStruct((bsz, length, width), dt)

    tab = pl.BlockSpec((tm, LANES), lambda b, i: (i, 0))
    cache_spec = pl.BlockSpec((1, 1, tm * HA, DA), lambda b, i: (layer, b, i, 0))
    cache_shape = jax.ShapeDtypeStruct((depth, bsz, length * HA, DA), F32)
    in_specs = [rows(d), _mod_spec(mod, tm, 0), _mod_spec(mod, tm, 1),
                _const_spec((1, d)), _const_spec((d, N_IN_PAD)), _const_spec((1, LANES)), tab, tab]
    args = [x, mod, mod, norm_g, w_re, bf_pad, cos, sin]
    aliases = {}
    if kv_prev is not None:
        in_specs += [pl.BlockSpec(memory_space=pl.ANY)] * 2
        args += list(kv_prev)
        aliases = {8: 0, 9: 1}
    return pl.pallas_call(
        functools.partial(_in_kernel, tm=tm, aliased=kv_prev is not None),
        grid=grid,
        in_specs=in_specs,
        out_specs=[cache_spec, cache_spec, rows(WA), rows(WA), rows(WA),
                   pl.BlockSpec((1, SUBLANES, tm), lambda b, i: (b, 0, i)),
                   rows(WB), rows(WB), rows(WB), rows(WB), rows(DC)],
        out_shape=[cache_shape, cache_shape, shp(WA, BF16), shp(WA, BF16), shp(WA, BF16),
                   jax.ShapeDtypeStruct((bsz, SUBLANES, length), F32),
                   shp(WB, BF16), shp(WB, BF16), shp(WB, BF16), shp(WB, BF16), shp(DC, F32)],
        input_output_aliases=aliases,
        compiler_params=_params(("arbitrary", "arbitrary")),
        name="in_proj",
    )(*args)


def _cumsum_kernel(x_ref, o_ref):
    x = x_ref[0]
    length = x.shape[1]
    lane = lax.broadcasted_iota(jnp.int32, x.shape, 1)
    s = 1
    while s < length:
        x = x + jnp.where(lane >= s, pltpu.roll(x, s, 1), 0.0)
        s *= 2
    o_ref[0] = x * LOG2E


def _cumsum(lft):
    bsz, r, length = lft.shape
    return pl.pallas_call(
        _cumsum_kernel,
        grid=(bsz,),
        in_specs=[pl.BlockSpec((1, r, length), lambda b: (b, 0, 0))],
        out_specs=pl.BlockSpec((1, r, length), lambda b: (b, 0, 0)),
        out_shape=jax.ShapeDtypeStruct(lft.shape, F32),
        compiler_params=_params(("arbitrary",)),
        name="logf_cumsum",
    )(lft)


def _fox_kernel(q_ref, k_ref, v_ref, ck_ref, o_ref, *, tq):
    qi = pl.program_id(1)
    row = lax.broadcasted_iota(jnp.int32, (tq, tq), 0)
    col = lax.broadcasted_iota(jnp.int32, (tq, tq), 1)
    causal = row >= col
    ones = jnp.ones((tq, DA), BF16)

    def step(kb, carry, masked):
        off = pl.multiple_of(kb * tq, tq)
        out = []
        for h in range(HA):
            m, acc = carry[h]
            sl = slice(h * DA, (h + 1) * DA)
            k = k_ref[0, pl.ds(off, tq), sl]
            v1 = jnp.concatenate([v_ref[0, pl.ds(off, tq), sl], ones], axis=1)
            s = _dot_nt(q_ref[0, :, sl], k) - ck_ref[0, kb, h:h + 1, :]
            if masked:
                s = jnp.where(causal, s, NEG)
            m_new = jnp.maximum(m, jnp.max(s, axis=-1, keepdims=True))
            p = jnp.exp2(s - m_new)
            acc = jnp.exp2(m - m_new) * acc + _dot(p.astype(BF16), v1)
            out.append((m_new, acc))
        return tuple(out)

    init = tuple((jnp.full((tq, 1), NEG, F32), jnp.zeros((tq, 2 * DA), F32)) for _ in range(HA))
    carry = lax.fori_loop(0, qi, functools.partial(step, masked=False), init)
    carry = step(qi, carry, True)
    for h in range(HA):
        acc = carry[h][1]
        o_ref[0, :, h * DA:(h + 1) * DA] = (acc[:, 0:DA] / acc[:, DA:2 * DA]).astype(BF16)


def _fox_prompt(qa, ka, va, cum, tq):
    bsz, length, _ = qa.shape
    nb = length // tq
    ck = cum.reshape(bsz, SUBLANES, nb, tq).transpose(0, 2, 1, 3)
    full = pl.BlockSpec((1, length, WA), lambda b, i: (b, 0, 0))
    return pl.pallas_call(
        functools.partial(_fox_kernel, tq=tq),
        grid=(bsz, nb),
        in_specs=[pl.BlockSpec((1, tq, WA), lambda b, i: (b, i, 0)), full, full,
                  pl.BlockSpec((1, nb, SUBLANES, tq), lambda b, i: (b, 0, 0, 0))],
        out_specs=pl.BlockSpec((1, tq, WA), lambda b, i: (b, i, 0)),
        out_shape=jax.ShapeDtypeStruct((bsz, length, WA), BF16),
        compiler_params=_params(("arbitrary", "arbitrary")),
        name="fox_prompt",
    )(qa, ka, va, ck)


def _head_mean(z, avg):
    hi = z.astype(BF16)
    lo = (z - hi.astype(F32)).astype(BF16)
    return _dot(hi, avg) + _dot(lo, avg)


def _group_avg_matrix():
    r = lax.broadcasted_iota(jnp.int32, (WB, WB), 0) // DVB
    c = lax.broadcasted_iota(jnp.int32, (WB, WB), 1) // DVB
    return r == c


def _lane_log_g(shape, axis):
    head = lax.broadcasted_iota(jnp.int32, shape, axis) // DVB
    lg = jnp.full(shape, _LOG_G[HB - 1], F32)
    for h in range(HB - 1):
        lg = jnp.where(head == h, _LOG_G[h], lg)
    return lg


def _ret_kernel(q_ref, k_ref, v_ref, gate_ref, gn_ref, o_ref, so_ref, s_ref, *, chunk):
    ci = pl.program_id(1)

    @pl.when(ci == 0)
    def _():
        s_ref[...] = jnp.zeros_like(s_ref)

    q = q_ref[0]
    k = k_ref[0]
    v = v_ref[0]
    head = lax.broadcasted_iota(jnp.int32, (1, WB), 1) // DVB
    r_i = lax.broadcasted_iota(jnp.int32, (chunk, chunk), 0)
    c_i = lax.broadcasted_iota(jnp.int32, (chunk, chunk), 1)
    rel = (r_i - c_i).astype(F32)
    o = jnp.zeros((chunk, WB), F32)
    for h in range(HB):
        decay = jnp.where(rel >= 0, jnp.exp(_LOG_G[h] * jnp.maximum(rel, 0.0)), 0.0)
        mh = head == h
        att = _dot_nt(jnp.where(mh, q, jnp.zeros_like(q)), k) * decay
        o = o + jnp.where(mh, _dot(att.astype(BF16), v), 0.0)

    lg = _lane_log_g((chunk, WB), 1)
    idx = lax.broadcasted_iota(jnp.int32, (chunk, WB), 0).astype(F32)
    xi = jnp.exp(lg * (idx + 1.0))
    zeta = jnp.exp(lg * (chunk - 1.0 - idx))
    s_old = s_ref[...]
    o = o + _dot(q, s_old.astype(BF16)) * xi
    kz_t = (k.astype(F32) * zeta).T.astype(BF16)
    same_head = _group_avg_matrix()
    g_chunk = jnp.exp(_lane_log_g((WB, WB), 0) * float(chunk))
    s_ref[...] = jnp.where(same_head, g_chunk * s_old + _dot(kz_t, v), 0.0)

    avg = jnp.where(same_head, 1.0 / DVB, 0.0).astype(BF16)
    xc = o - _head_mean(o, avg)
    var = _head_mean(xc * xc, avg)
    r = xc * lax.rsqrt(var + EPS) * gn_ref[...]
    o_ref[0] = (r * gate_ref[0].astype(F32)).astype(BF16)

    @pl.when(ci == pl.num_programs(1) - 1)
    def _():
        so_ref[0] = s_ref[...]


def _ret_prompt(rq, rk, rv, gate, gn, chunk):
    bsz, length, _ = rq.shape
    rows = pl.BlockSpec((1, chunk, WB), lambda b, i: (b, i, 0))
    return pl.pallas_call(
        functools.partial(_ret_kernel, chunk=chunk),
        grid=(bsz, length // chunk),
        in_specs=[rows, rows, rows, rows, _const_spec((1, WB))],
        out_specs=[rows, pl.BlockSpec((1, WB, WB), lambda b, i: (b, 0, 0))],
        out_shape=[jax.ShapeDtypeStruct((bsz, length, WB), BF16),
                   jax.ShapeDtypeStruct((bsz, WB, WB), F32)],
        scratch_shapes=[pltpu.VMEM((WB, WB), F32)],
        compiler_params=_params(("arbitrary", "arbitrary")),
        name="ret_prompt",
    )(rq, rk, rv, gate, gn)


_CONV_PAD = 32


def _layernorm(x, g, b):
    mu = jnp.mean(x, axis=-1, keepdims=True)
    xc = x - mu
    return xc * lax.rsqrt(jnp.mean(xc * xc, axis=-1, keepdims=True) + EPS) * g + b


def _conv_kernel(u_ref, w_ref, b_ref, lg_ref, lb_ref, o_ref, st_ref, ext_ref, *, tm):
    i = pl.program_id(1)

    @pl.when(i == 0)
    def _():
        ext_ref[0:_CONV_PAD, :] = jnp.zeros((_CONV_PAD, DC), F32)

    ext_ref[_CONV_PAD:_CONV_PAD + tm, :] = u_ref[0]
    acc = jnp.zeros((tm, DC), F32) + b_ref[...]
    base = _CONV_PAD - (CONV_W - 1)
    for j in range(CONV_W):
        acc = acc + w_ref[j:j + 1, :] * ext_ref[base + j:base + j + tm, :]
    o_ref[0] = _silu(_layernorm(acc, lg_ref[...], lb_ref[...])).astype(BF16)

    @pl.when(i == pl.num_programs(1) - 1)
    def _():
        st_ref[0] = ext_ref[tm + base:tm + _CONV_PAD, :]

    ext_ref[0:_CONV_PAD, :] = ext_ref[tm:tm + _CONV_PAD, :]


def _conv_prompt(u, w, b, lg, lb, tm):
    bsz, length, _ = u.shape
    rows = pl.BlockSpec((1, tm, DC), lambda bb, i: (bb, i, 0))
    return pl.pallas_call(
        functools.partial(_conv_kernel, tm=tm),
        grid=(bsz, length // tm),
        in_specs=[rows, _const_spec((CONV_W, DC)), _const_spec((1, DC)),
                  _const_spec((1, DC)), _const_spec((1, DC))],
        out_specs=[rows, pl.BlockSpec((1, CONV_W - 1, DC), lambda bb, i: (bb, 0, 0))],
        out_shape=[jax.ShapeDtypeStruct((bsz, length, DC), BF16),
                   jax.ShapeDtypeStruct((bsz, CONV_W - 1, DC), F32)],
        scratch_shapes=[pltpu.VMEM((tm + _CONV_PAD, DC), F32)],
        compiler_params=_params(("arbitrary", "arbitrary")),
        name="conv_prompt",
    )(u, w, b, lg, lb)


_FF_CHUNK = 256
_FF_BUFS = 4


def _ffn_kernel(*refs, tm, step_mode):
    if step_mode:
        (x_ref, oa_ref, ob_ref, oc_ref, g1_ref, sh2_ref, sc2_ref, g2_ref, n_ref,
         wo_ref, wup_ref, cw_ref, cb_ref, wdn_ref, st_ref, y_ref, ns_ref, act_ref) = refs
    else:
        (x_ref, oa_ref, ob_ref, oc_ref, g1_ref, sh2_ref, sc2_ref, g2_ref, n_ref,
         wo_ref, wup_ref, cw_ref, cb_ref, wdn_ref, y_ref, ns_ref, act_ref, ctx_ref, bufs_ref) = refs
        i = pl.program_id(1)

        @pl.when(i == 0)
        def _():
            ctx_ref[...] = jnp.zeros_like(ctx_ref)

    mix = (_dot(oa_ref[0], wo_ref[0:WA, :]) + _dot(ob_ref[0], wo_ref[WA:WA + WB, :])
           + _dot(oc_ref[0], wo_ref[WA + WB:, :]))
    x1 = x_ref[0] + g1_ref[0] * (_rms(mix) * n_ref[1:2, :])
    h2 = ((_rms(x1) * n_ref[2:3, :]) * (1.0 + sc2_ref[0]) + sh2_ref[0]).astype(BF16)

    for c in range(D_FF // _FF_CHUNK):
        halves = []
        for half in range(2):
            c0 = half * D_FF + c * _FF_CHUNK
            cs = slice(c0, c0 + _FF_CHUNK)
            up = _dot(h2, wup_ref[:, cs])
            if step_mode:
                prev2 = st_ref[0, :, cs]
                prev1 = st_ref[0, :, 2 * D_FF + c0:2 * D_FF + c0 + _FF_CHUNK]
                ns_ref[0, :, cs] = prev1
                ns_ref[0, :, 2 * D_FF + c0:2 * D_FF + c0 + _FF_CHUNK] = up
            else:
                buf_ref = bufs_ref.at[(2 * c + half) % _FF_BUFS]
                buf_ref[0:SUBLANES, :] = ctx_ref[:, cs]
                buf_ref[SUBLANES:SUBLANES + tm, :] = up
                prev1 = buf_ref[SUBLANES - 1:SUBLANES - 1 + tm, :]
                prev2 = buf_ref[SUBLANES - 2:SUBLANES - 2 + tm, :]
                ctx_ref[:, cs] = buf_ref[tm:tm + SUBLANES, :]
            halves.append(cw_ref[0:1, cs] * prev2 + cw_ref[1:2, cs] * prev1
                          + cw_ref[2:3, cs] * up + cb_ref[:, cs])
        act_ref[:, c * _FF_CHUNK:(c + 1) * _FF_CHUNK] = (_silu(halves[0]) * halves[1]).astype(BF16)
    f = _dot(act_ref[...], wdn_ref[...])
    y_ref[0] = x1 + g2_ref[0] * (_rms(f) * n_ref[3:4, :])

    if not step_mode:
        @pl.when(i == pl.num_programs(1) - 1)
        def _():
            ns_ref[0] = ctx_ref[SUBLANES - 2:SUBLANES, :]


def _out_ffn(x, oa, ob, oc, mod, norms, wo, wup, cw, cb, wdn, tm, state=None, layer=0):
    bsz, length, d = x.shape
    step_mode = state is not None

    def rows(width):
        return pl.BlockSpec((1, tm, width), lambda b, i: (b, i, 0))

    in_specs = [rows(d), rows(WA), rows(WB), rows(DC),
                _mod_spec(mod, tm, 2), _mod_spec(mod, tm, 3), _mod_spec(mod, tm, 4), _mod_spec(mod, tm, 5),
                _const_spec((4, d)), _const_spec((d, d)), _const_spec((d, 2 * D_FF)),
                _const_spec((3, 2 * D_FF)), _const_spec((1, 2 * D_FF)), _const_spec((D_FF, d))]
    args = [x, oa, ob, oc, mod, mod, mod, mod, norms, wo, wup, cw, cb, wdn]
    if step_mode:
        in_specs.append(pl.BlockSpec((1, tm, 4 * D_FF), lambda b, i: (layer, i, 0)))
        args.append(state)
        ns_spec = pl.BlockSpec((1, tm, 4 * D_FF), lambda b, i: (b, i, 0))
        ns_shape = jax.ShapeDtypeStruct((bsz, length, 4 * D_FF), F32)
        scratch = [pltpu.VMEM((tm, D_FF), BF16)]
    else:
        ns_spec = pl.BlockSpec((1, 2, 2 * D_FF), lambda b, i: (b, 0, 0))
        ns_shape = jax.ShapeDtypeStruct((bsz, 2, 2 * D_FF), F32)
        scratch = [pltpu.VMEM((tm, D_FF), BF16), pltpu.VMEM((SUBLANES, 2 * D_FF), F32),
                   pltpu.VMEM((_FF_BUFS, tm + SUBLANES, _FF_CHUNK), F32)]
    return pl.pallas_call(
        functools.partial(_ffn_kernel, tm=tm, step_mode=step_mode),
        grid=(bsz, length // tm),
        in_specs=in_specs,
        out_specs=[rows(d), ns_spec],
        out_shape=[jax.ShapeDtypeStruct((bsz, length, d), F32), ns_shape],
        scratch_shapes=scratch,
        compiler_params=_params(("arbitrary", "arbitrary")),
        name="out_ffn_step" if step_mode else "out_ffn_seq",
    )(*args)


def _paged_kernel(pt_ref, q_ref, kn_ref, vn_ref, lfn_ref, *rest, n_pages):
    kp = rest[0:n_pages]
    vp = rest[n_pages:2 * n_pages]
    lp = rest[2 * n_pages:3 * n_pages]
    o_ref = rest[3 * n_pages]
    pw = PAGE_SIZE * HA
    total = n_pages * pw

    q = q_ref[0]
    qb = q.astype(BF16)
    rowh = lax.broadcasted_iota(jnp.int32, (SUBLANES, total), 0)
    lane = lax.broadcasted_iota(jnp.int32, (SUBLANES, total), 1)
    own = (lane % HA) == rowh

    lf = jnp.broadcast_to(jnp.concatenate([r[0, 0] for r in lp], axis=1), (SUBLANES, total))
    suf = lf
    s = HA
    while s < total:
        suf = suf + jnp.where(lane + s < total, pltpu.roll(suf, total - s, 1), 0.0)
        s *= 2
    bias = (suf - lf + lfn_ref[0]) * LOG2E

    logits = jnp.concatenate([_dot_nt(qb, r[0, 0].astype(BF16)) for r in kp], axis=1) + bias
    logits = jnp.where(own, logits, NEG)
    s_new = jnp.sum(q * kn_ref[0], axis=-1, keepdims=True)
    m = jnp.maximum(jnp.max(logits, axis=-1, keepdims=True), s_new)
    p = jnp.exp2(logits - m)
    p_new = jnp.exp2(s_new - m)
    l = jnp.sum(p, axis=-1, keepdims=True) + p_new
    pb = p.astype(BF16)
    acc = p_new * vn_ref[0]
    for j in range(n_pages):
        acc = acc + _dot(pb[:, j * pw:(j + 1) * pw], vp[j][0, 0].astype(BF16))
    o_ref[0] = acc / l


def _fox_sample(page_table, q, k_new, v_new, lf_new, pool_k, pool_v, pool_lf, layer):
    n, n_pages = page_table.shape
    pt = page_table.reshape(-1)
    pw = PAGE_SIZE * HA
    row = pl.BlockSpec((1, SUBLANES, DA), lambda i, pt: (i, 0, 0))

    def page(width_shape, j):
        return pl.BlockSpec((1, 1) + width_shape, lambda i, pt, j=j: (layer, pt[i * n_pages + j], 0, 0))

    in_specs = ([row, row, row, pl.BlockSpec((1, SUBLANES, 1), lambda i, pt: (i, 0, 0))]
                + [page((pw, DA), j) for j in range(n_pages)]
                + [page((pw, DA), j) for j in range(n_pages)]
                + [page((1, pw), j) for j in range(n_pages)])
    grid_spec = pltpu.PrefetchScalarGridSpec(
        num_scalar_prefetch=1, grid=(n,), in_specs=in_specs,
        out_specs=pl.BlockSpec((1, SUBLANES, DA), lambda i, pt: (i, 0, 0)))
    return pl.pallas_call(
        functools.partial(_paged_kernel, n_pages=n_pages),
        grid_spec=grid_spec,
        out_shape=jax.ShapeDtypeStruct((n, SUBLANES, DA), F32),
        compiler_params=_params(("arbitrary",)),
        name="fox_sample",
    )(pt, q, k_new, v_new, lf_new, *([pool_k] * n_pages), *([pool_v] * n_pages), *([pool_lf] * n_pages))


def _sret_kernel(q_ref, k_ref, v_ref, gate_ref, gn_ref, s0_ref, o_ref, sn_ref):
    q = q_ref[0]
    k = k_ref[0]
    v = v_ref[0]
    hw = DKB * DVB
    lane = lax.broadcasted_iota(jnp.int32, (DKB, hw), 1)
    rowi = lax.broadcasted_iota(jnp.int32, (DKB, hw), 0)
    rep = (lane // DVB == rowi).astype(BF16)
    til = (lane % DVB == rowi).astype(BF16)
    outs = []
    for h in range(HB):
        sl = slice(h * DKB, (h + 1) * DKB)
        g = math.exp(_LOG_G[h])
        sn = g * s0_ref[0, :, h * hw:(h + 1) * hw] + _dot(k[:, sl], rep) * _dot(v[:, sl], til)
        sn_ref[0, :, h * hw:(h + 1) * hw] = sn
        prod = _dot(q[:, sl], rep) * sn
        acc = prod[:, 0:LANES]
        for j in range(1, hw // LANES):
            acc = acc + prod[:, j * LANES:(j + 1) * LANES]
        oh = acc[:, 0:DVB] + acc[:, DVB:LANES]
        mu = jnp.mean(oh, axis=-1, keepdims=True)
        xc = oh - mu
        outs.append(xc * lax.rsqrt(jnp.mean(xc * xc, axis=-1, keepdims=True) + EPS))
    r = jnp.concatenate(outs, axis=1) * gn_ref[...]
    o_ref[0] = (r * gate_ref[0].astype(F32)).astype(BF16)


def _ret_sample(rq, rk, rv, gate, gn, state, layer, nb):
    _, n, _ = rq.shape
    sw = HB * DKB * DVB
    rows = pl.BlockSpec((1, nb, WB), lambda i: (0, i, 0))
    return pl.pallas_call(
        _sret_kernel,
        grid=(n // nb,),
        in_specs=[rows, rows, rows, rows, _const_spec((1, WB)),
                  pl.BlockSpec((1, nb, sw), lambda i: (layer, i, 0))],
        out_specs=[rows, pl.BlockSpec((1, nb, sw), lambda i: (0, i, 0))],
        out_shape=[jax.ShapeDtypeStruct((1, n, WB), BF16), jax.ShapeDtypeStruct((1, n, sw), F32)],
        compiler_params=_params(("arbitrary",)),
        name="ret_sample",
    )(rq, rk, rv, gate, gn, state)


def _sconv_kernel(ctx_ref, u_ref, w_ref, b_ref, lg_ref, lb_ref, o_ref, ns_ref):
    ctx = ctx_ref[0]
    u = u_ref[...]
    w = w_ref[...]
    cv = (jnp.sum(ctx * w[0:CONV_W - 1, :][None], axis=1, keepdims=True)
          + u * w[CONV_W - 1:CONV_W, :][None] + b_ref[...][None])
    o_ref[...] = _silu(_layernorm(cv, lg_ref[...][None], lb_ref[...][None])).astype(BF16)
    ns_ref[0, :, 0:CONV_W - 2, :] = ctx[:, 1:CONV_W - 1, :]
    ns_ref[0, :, CONV_W - 2:CONV_W - 1, :] = u


def _conv_sample(state, u, w, b, lg, lb, layer, nb):
    n = u.shape[0]
    return pl.pallas_call(
        _sconv_kernel,
        grid=(n // nb,),
        in_specs=[pl.BlockSpec((1, nb, CONV_W - 1, DC), lambda i: (layer, i, 0, 0)),
                  pl.BlockSpec((nb, 1, DC), lambda i: (i, 0, 0)),
                  _const_spec((CONV_W, DC)), _const_spec((1, DC)), _const_spec((1, DC)), _const_spec((1, DC))],
        out_specs=[pl.BlockSpec((nb, 1, DC), lambda i: (i, 0, 0)),
                   pl.BlockSpec((1, nb, CONV_W - 1, DC), lambda i: (0, i, 0, 0))],
        out_shape=[jax.ShapeDtypeStruct((n, 1, DC), BF16),
                   jax.ShapeDtypeStruct((1, n, CONV_W - 1, DC), F32)],
        compiler_params=_params(("arbitrary",)),
        name="conv_sample",
    )(state, u, w, b, lg, lb)


def _reorder_w_in(w):
    o = 3 * WA
    main = jnp.concatenate([w[:, :o], w[:, o + HA:]], axis=1)
    tail = jnp.pad(w[:, o:o + HA], ((0, 0), (0, LANES - HA)))
    return jnp.concatenate([main, tail], axis=1).astype(BF16)


def _pick(total, pref):
    t = min(total, pref)
    assert total % t == 0
    return t


def kernel(x_prompt, x_sample, cache_k, cache_v, cache_logf, state_ret, state_conv, state_ffn_conv, page_table,
           c_prompt, c_sample, ada_w, ada_b, norms, w_in, b_f, ret_gn_g, conv_w, conv_b, conv_ln_g, conv_ln_b,
           w_o, ffn_up, ffn_conv_w, ffn_conv_b, ffn_down):
    depth = ada_w.shape[0]
    bp, sp, d = x_prompt.shape
    ns, ts, _ = x_sample.shape
    assert ts == 1 and d == D_MODEL
    n_pool = cache_k.shape[1]
    n_pages = page_table.shape[1]

    n_c = ns + bp
    n_c_pad = -(-n_c // SUBLANES) * SUBLANES
    c_all = jnp.pad(jnp.concatenate([c_sample, c_prompt], axis=0), ((0, n_c_pad - n_c), (0, 0)))
    mod_all = _ada_mod(c_all, ada_w, ada_b)

    cos_p, sin_p = _rope_tables(sp, 0, 1)
    cos_s, sin_s = _rope_tables(ns, n_pages * PAGE_SIZE, 0)

    pool_k = cache_k.reshape(depth, n_pool, PAGE_SIZE * HA, DA)
    pool_v = cache_v.reshape(depth, n_pool, PAGE_SIZE * HA, DA)
    pool_lf = cache_logf.reshape(depth, n_pool, 1, PAGE_SIZE * HA)
    st_ret = state_ret.reshape(depth, ns, HB * DKB * DVB)
    st_ffn = state_ffn_conv.reshape(depth, ns, 4 * D_FF)

    tm_p = _pick(sp, 512)
    tq = _pick(sp, 512)
    chunk = _pick(sp, 128)
    nb_s = _pick(ns, 32)

    xp = x_prompt
    xs = x_sample.reshape(1, ns, d)
    p_out, s_out = [], []
    kv_p = kv_s = None

    def heads_on_rows(a):
        return jnp.pad(a.reshape(ns, HA, DA), ((0, 0), (0, SUBLANES - HA), (0, 0)))

    for l in range(depth):
        w_re = _reorder_w_in(w_in[l])
        bf_pad = jnp.pad(b_f[l], (0, LANES - HA)).reshape(1, LANES)
        wo = w_o[l].astype(BF16)
        wup = ffn_up[l].astype(BF16)
        wdn = ffn_down[l].astype(BF16)
        n0 = norms[l, 0:1]
        gn = ret_gn_g[l].reshape(1, WB)
        cb = conv_b[l].reshape(1, DC)
        clg = conv_ln_g[l].reshape(1, DC)
        clb = conv_ln_b[l].reshape(1, DC)
        fcb = ffn_conv_b[l].reshape(1, 2 * D_FF)
        mod_s = mod_all[l, 0:ns].reshape(1, ns, 6 * d)
        mod_p = mod_all[l, ns:ns + bp].reshape(bp, 1, 6 * d)

        ko, vo, qa, ka, va, lft, rq, rk, rv, gate, u = _in_proj(
            xp, mod_p, n0, w_re, bf_pad, cos_p, sin_p, tm_p, depth, l, kv_p)
        kv_p = (ko, vo)
        cum = _cumsum(lft)
        oa = _fox_prompt(qa, ka, va, cum, tq)
        ob, s_bd = _ret_prompt(rq, rk, rv, gate, gn, chunk)
        oc, conv_st = _conv_prompt(u, conv_w[l], cb, clg, clb, tm_p)
        xp, ffn_st = _out_ffn(xp, oa, ob, oc, mod_p, norms[l], wo, wup, ffn_conv_w[l], fcb, wdn, tm_p)
        s5 = s_bd.reshape(bp, HB, DKB, HB, DVB)
        ret_st = jnp.stack([s5[:, h, :, h, :] for h in range(HB)], axis=1)
        p_out.append((lft[:, 0:HA, :].transpose(0, 2, 1), ret_st, conv_st, ffn_st))

        ko, vo, qa, ka, va, lft, rq, rk, rv, gate, u = _in_proj(
            xs, mod_s, n0, w_re, bf_pad, cos_s, sin_s, ns, depth, l, kv_s)
        kv_s = (ko, vo)
        lfn = lft[0].T.reshape(ns, SUBLANES, 1)
        oa = _fox_sample(page_table, heads_on_rows(qa.astype(F32)), heads_on_rows(ko[l, 0]),
                         heads_on_rows(vo[l, 0]), lfn, pool_k, pool_v, pool_lf, l)
        oa = oa[:, 0:HA, :].reshape(1, ns, WA).astype(BF16)
        ob, ret_new = _ret_sample(rq, rk, rv, gate, gn, st_ret, l, nb_s)
        oc, conv_new = _conv_sample(state_conv, u.reshape(ns, 1, DC), conv_w[l], cb, clg, clb, l, nb_s)
        xs, ffn_new = _out_ffn(xs, oa, ob, oc.reshape(1, ns, DC), mod_s, norms[l],
                               wo, wup, ffn_conv_w[l], fcb, wdn, ns, state=st_ffn, layer=l)
        s_out.append((lft[0, 0:HA, :].T.reshape(ns, 1, HA), ret_new.reshape(ns, HB, DKB, DVB),
                      conv_new[0], ffn_new.reshape(ns, 2, 2 * D_FF)))

    plf, pret, pconv, pffn = [jnp.stack(a) for a in zip(*p_out)]
    slf, sret, sconv, sffn = [jnp.stack(a) for a in zip(*s_out)]
    pk, pv = [a.reshape(depth, bp, sp, HA, DA) for a in kv_p]
    sk, sv = [a.reshape(depth, ns, 1, HA, DA) for a in kv_s]
    return (xp, xs.reshape(ns, 1, d), pk, pv, plf, pret, pconv, pffn, sk, sv, slf, sret, sconv, sffn)
```

```python
import functools
import math

import jax
import jax.numpy as jnp
import numpy as np
from jax import lax
from jax.experimental import pallas as pl
from jax.experimental.pallas import tpu as pltpu

F32 = jnp.float32
BF16 = jnp.bfloat16

D_MODEL = 1024
HA, DA = 4, 128
WA = HA * DA
HB, DKB, DVB = 4, 64, 64
WB = HB * DVB
DC = 256
CONV_W = 31
D_FF = 2816
PAGE_SIZE = 128
ROPE_BASE = 10000.0
EPS = 1e-6
NEG = -1e30
LOG2E = math.log2(math.e)

LANES = 128
SUBLANES = 8
VMEM_LIMIT = 56 * 1024 * 1024

_OQ, _OK, _OV = 0, 512, 1024
_OBQ, _OBK, _OBV, _OBG = 1536, 1792, 2048, 2304
_OCA, _OCB, _OAF = 2560, 2816, 3072
N_IN_PAD = 3200

_LOG_G = [math.log1p(-(2.0 ** (-5.0 - h))) for h in range(HB)]


def _sigmoid(x):
    return 1.0 / (1.0 + jnp.exp(-x))


def _silu(x):
    return x * _sigmoid(x)


def _log_sigmoid(x):
    return jnp.minimum(x, 0.0) - jnp.log1p(jnp.exp(-jnp.abs(x)))


def _rms(x):
    return x * lax.rsqrt(jnp.mean(x * x, axis=-1, keepdims=True) + EPS)


def _dot(a, b):
    return jnp.dot(a, b, preferred_element_type=F32)


def _dot_nt(a, b):
    return lax.dot_general(a, b, (((1,), (1,)), ((), ())), preferred_element_type=F32)


def _params(sem):
    return pltpu.CompilerParams(dimension_semantics=sem, vmem_limit_bytes=VMEM_LIMIT)


def _const_spec(shape):
    nd = len(shape)
    return pl.BlockSpec(shape, lambda *_: (0,) * nd, pipeline_mode=pl.Buffered(1))


def _ada_kernel(c_ref, w_ref, b_ref, o_ref):
    s = _silu(c_ref[...]).astype(BF16)
    o_ref[0] = _dot(s, w_ref[0].astype(BF16)) + b_ref[0]


def _ada_mod(c_all, ada_w, ada_b):
    depth, d, n6 = ada_w.shape
    r = c_all.shape[0]
    tn = 1536
    return pl.pallas_call(
        _ada_kernel,
        grid=(depth, n6 // tn),
        in_specs=[pl.BlockSpec((r, d), lambda l, j: (0, 0)),
                  pl.BlockSpec((1, d, tn), lambda l, j: (l, 0, j)),
                  pl.BlockSpec((1, 1, tn), lambda l, j: (l, 0, j))],
        out_specs=pl.BlockSpec((1, r, tn), lambda l, j: (l, 0, j)),
        out_shape=jax.ShapeDtypeStruct((depth, r, n6), F32),
        compiler_params=_params(("arbitrary", "arbitrary")),
        name="ada_mod",
    )(c_all, ada_w, ada_b.reshape(depth, 1, n6))


def _rope_kernel(inv_ref, cos_ref, sin_ref, *, pos0, stride, tm):
    i = pl.program_id(0)
    row = lax.broadcasted_iota(jnp.int32, (tm, LANES), 0) + i * tm
    lane = lax.broadcasted_iota(jnp.int32, (tm, LANES), 1)
    ang = (pos0 + stride * row).astype(F32) * inv_ref[...]
    cos_ref[...] = jnp.cos(ang)
    s = jnp.sin(ang)
    sin_ref[...] = jnp.where((lane % DKB) < DKB // 2, -s, s)


def _rope_tables(length, pos0, stride):
    half = DKB // 2
    inv = ROPE_BASE ** (-jnp.arange(half, dtype=F32) / half)
    inv = jnp.tile(inv, LANES // half).reshape(1, LANES)
    tm = min(length, 1024)
    out = jax.ShapeDtypeStruct((length, LANES), F32)
    return pl.pallas_call(
        functools.partial(_rope_kernel, pos0=pos0, stride=stride, tm=tm),
        grid=(length // tm,),
        in_specs=[pl.BlockSpec((1, LANES), lambda i: (0, 0))],
        out_specs=[pl.BlockSpec((tm, LANES), lambda i: (i, 0))] * 2,
        out_shape=[out, out],
        compiler_params=_params(("arbitrary",)),
        name="rope_tables",
    )(inv)


def _in_kernel(*refs, tm, aliased):
    if aliased:
        refs = refs[:8] + refs[10:]
    (x_ref, sh_ref, sc_ref, g_ref, w_ref, bf_ref, cos_ref, sin_ref,
     ko_ref, vo_ref, qa_ref, ka_ref, va_ref, lft_ref, rq_ref, rk_ref, rv_ref, gate_ref, u_ref) = refs
    x = x_ref[0]
    h = _rms(x) * g_ref[...]
    hb = (h * (1.0 + sc_ref[0]) + sh_ref[0]).astype(BF16)

    def proj(off, width):
        return _dot(hb, w_ref[:, off:off + width])

    qa_ref[0] = (proj(_OQ, WA) * (DA ** -0.5 * LOG2E)).astype(BF16)
    k = proj(_OK, WA)
    ka_ref[0] = k.astype(BF16)
    v = proj(_OV, WA)
    va_ref[0] = v.astype(BF16)
    for hd in range(HA):
        ko_ref[0, 0, pl.ds(hd, tm, stride=HA), :] = k[:, hd * DA:(hd + 1) * DA]
        vo_ref[0, 0, pl.ds(hd, tm, stride=HA), :] = v[:, hd * DA:(hd + 1) * DA]

    lf = _log_sigmoid(proj(_OAF, LANES) + bf_ref[...])
    lft_ref[0] = lf.T[0:SUBLANES, :]

    cos = cos_ref[...]
    sin = sin_ref[...]
    lane = lax.broadcasted_iota(jnp.int32, cos.shape, 1)
    first = (lane % DKB) < DKB // 2

    def rope(t):
        partner = jnp.where(first, pltpu.roll(t, LANES - DKB // 2, 1), pltpu.roll(t, DKB // 2, 1))
        return t * cos + partner * sin

    bq = proj(_OBQ, WB)
    bk = proj(_OBK, WB)
    for j in range(WB // LANES):
        sl = slice(j * LANES, (j + 1) * LANES)
        rq_ref[0, :, sl] = rope(bq[:, sl]).astype(BF16)
        rk_ref[0, :, sl] = (rope(bk[:, sl]) * (DKB ** -0.5)).astype(BF16)
    rv_ref[0] = proj(_OBV, WB).astype(BF16)
    gate_ref[0] = _silu(proj(_OBG, WB)).astype(BF16)
    u_ref[0] = proj(_OCA, DC) * _sigmoid(proj(_OCB, DC))


def _mod_spec(mod, tm, col):
    rows = mod.shape[1]
    if rows == 1:
        return pl.BlockSpec((1, 1, D_MODEL), lambda b, i: (b, 0, col))
    return pl.BlockSpec((1, tm, D_MODEL), lambda b, i: (b, i, col))


def _in_proj(x, mod, norm_g, w_re, bf_pad, cos, sin, tm, depth, layer, kv_prev):
    bsz, length, d = x.shape
    grid = (bsz, length // tm)

    def rows(width):
        return pl.BlockSpec((1, tm, width), lambda b, i: (b, i, 0))

    def shp(width, dt):
        return jax.ShapeDtypeStruct((bsz, length, width), dt)

    tab = pl.BlockSpec((tm, LANES), lambda b, i: (i, 0))
    cache_spec = pl.BlockSpec((1, 1, tm * HA, DA), lambda b, i: (layer, b, i, 0))
    cache_shape = jax.ShapeDtypeStruct((depth, bsz, length * HA, DA), F32)
    in_specs = [rows(d), _mod_spec(mod, tm, 0), _mod_spec(mod, tm, 1),
                _const_spec((1, d)), _const_spec((d, N_IN_PAD)), _const_spec((1, LANES)), tab, tab]
    args = [x, mod, mod, norm_g, w_re, bf_pad, cos, sin]
    aliases = {}
    if kv_prev is not None:
        in_specs += [pl.BlockSpec(memory_space=pl.ANY)] * 2
        args += list(kv_prev)
        aliases = {8: 0, 9: 1}
    return pl.pallas_call(
        functools.partial(_in_kernel, tm=tm, aliased=kv_prev is not None),
        grid=grid,
        in_specs=in_specs,
        out_specs=[cache_spec, cache_spec, rows(WA), rows(WA), rows(WA),
                   pl.BlockSpec((1, SUBLANES, tm), lambda b, i: (b, 0, i)),
                   rows(WB), rows(WB), rows(WB), rows(WB), rows(DC)],
        out_shape=[cache_shape, cache_shape, shp(WA, BF16), shp(WA, BF16), shp(WA, BF16),
                   jax.ShapeDtypeStruct((bsz, SUBLANES, length), F32),
                   shp(WB, BF16), shp(WB, BF16), shp(WB, BF16), shp(WB, BF16), shp(DC, F32)],
        input_output_aliases=aliases,
        compiler_params=_params(("arbitrary", "arbitrary")),
        name="in_proj",
    )(*args)


def _cumsum_kernel(x_ref, o_ref):
    x = x_ref[0]
    length = x.shape[1]
    lane = lax.broadcasted_iota(jnp.int32, x.shape, 1)
    s = 1
    while s < length:
        x = x + jnp.where(lane >= s, pltpu.roll(x, s, 1), 0.0)
        s *= 2
    o_ref[0] = x * LOG2E


def _cumsum(lft):
    bsz, r, length = lft.shape
    return pl.pallas_call(
        _cumsum_kernel,
        grid=(bsz,),
        in_specs=[pl.BlockSpec((1, r, length), lambda b: (b, 0, 0))],
        out_specs=pl.BlockSpec((1, r, length), lambda b: (b, 0, 0)),
        out_shape=jax.ShapeDtypeStruct(lft.shape, F32),
        compiler_params=_params(("arbitrary",)),
        name="logf_cumsum",
    )(lft)


def _fox_kernel(q_ref, k_ref, v_ref, ck_ref, o_ref, *, tq):
    qi = pl.program_id(1)
    row = lax.broadcasted_iota(jnp.int32, (tq, tq), 0)
    col = lax.broadcasted_iota(jnp.int32, (tq, tq), 1)
    causal = row >= col
    ones = jnp.ones((tq, DA), BF16)

    def step(kb, carry, masked):
        off = pl.multiple_of(kb * tq, tq)
        out = []
        for h in range(HA):
            m, acc = carry[h]
            sl = slice(h * DA, (h + 1) * DA)
            k = k_ref[0, pl.ds(off, tq), sl]
            v1 = jnp.concatenate([v_ref[0, pl.ds(off, tq), sl], ones], axis=1)
            s = _dot_nt(q_ref[0, :, sl], k) - ck_ref[0, kb, h:h + 1, :]
            if masked:
                s = jnp.where(causal, s, NEG)
            m_new = jnp.maximum(m, jnp.max(s, axis=-1, keepdims=True))
            p = jnp.exp2(s - m_new)
            acc = jnp.exp2(m - m_new) * acc + _dot(p.astype(BF16), v1)
            out.append((m_new, acc))
        return tuple(out)

    init = tuple((jnp.full((tq, 1), NEG, F32), jnp.zeros((tq, 2 * DA), F32)) for _ in range(HA))
    carry = lax.fori_loop(0, qi, functools.partial(step, masked=False), init)
    carry = step(qi, carry, True)
    for h in range(HA):
        acc = carry[h][1]
        o_ref[0, :, h * DA:(h + 1) * DA] = (acc[:, 0:DA] / acc[:, DA:2 * DA]).astype(BF16)


def _fox_prompt(qa, ka, va, cum, tq):
    bsz, length, _ = qa.shape
    nb = length // tq
    ck = cum.reshape(bsz, SUBLANES, nb, tq).transpose(0, 2, 1, 3)
    full = pl.BlockSpec((1, length, WA), lambda b, i: (b, 0, 0), pipeline_mode=pl.Buffered(1))
    return pl.pallas_call(
        functools.partial(_fox_kernel, tq=tq),
        grid=(bsz, nb),
        in_specs=[pl.BlockSpec((1, tq, WA), lambda b, i: (b, i, 0)), full, full,
                  pl.BlockSpec((1, nb, SUBLANES, tq), lambda b, i: (b, 0, 0, 0))],
        out_specs=pl.BlockSpec((1, tq, WA), lambda b, i: (b, i, 0)),
        out_shape=jax.ShapeDtypeStruct((bsz, length, WA), BF16),
        compiler_params=_params(("arbitrary", "arbitrary")),
        name="fox_prompt",
    )(qa, ka, va, ck)


def _head_mean(z, avg):
    hi = z.astype(BF16)
    lo = (z - hi.astype(F32)).astype(BF16)
    return _dot(hi, avg) + _dot(lo, avg)


def _group_avg_matrix():
    r = lax.broadcasted_iota(jnp.int32, (WB, WB), 0) // DVB
    c = lax.broadcasted_iota(jnp.int32, (WB, WB), 1) // DVB
    return r == c


def _lane_log_g(shape, axis):
    head = lax.broadcasted_iota(jnp.int32, shape, axis) // DVB
    lg = jnp.full(shape, _LOG_G[HB - 1], F32)
    for h in range(HB - 1):
        lg = jnp.where(head == h, _LOG_G[h], lg)
    return lg


def _ret_kernel(q_ref, k_ref, v_ref, gate_ref, gn_ref, o_ref, so_ref,
                s_ref, dec_ref, xi_ref, zeta_ref, gc_ref, *, chunk, bsz):
    ci = pl.program_id(0)
    same_head = _group_avg_matrix()

    @pl.when(ci == 0)
    def _():
        s_ref[...] = jnp.zeros_like(s_ref)
        r_i = lax.broadcasted_iota(jnp.int32, (chunk, chunk), 0)
        c_i = lax.broadcasted_iota(jnp.int32, (chunk, chunk), 1)
        rel = (r_i - c_i).astype(F32)
        for h in range(HB):
            dec_ref[h] = jnp.where(rel >= 0, jnp.exp(_LOG_G[h] * jnp.maximum(rel, 0.0)), 0.0)
        lg = _lane_log_g((chunk, WB), 1)
        idx = lax.broadcasted_iota(jnp.int32, (chunk, WB), 0).astype(F32)
        xi_ref[...] = jnp.exp(lg * (idx + 1.0))
        zeta_ref[...] = jnp.exp(lg * (chunk - 1.0 - idx))
        gc_ref[...] = jnp.exp(_lane_log_g((WB, WB), 0) * float(chunk))

    head = lax.broadcasted_iota(jnp.int32, (1, WB), 1) // DVB
    avg = jnp.where(same_head, 1.0 / DVB, 0.0).astype(BF16)
    for b in range(bsz):
        q = q_ref[b]
        k = k_ref[b]
        v = v_ref[b]
        o = jnp.zeros((chunk, WB), F32)
        for h in range(HB):
            mh = head == h
            att = _dot_nt(jnp.where(mh, q, jnp.zeros_like(q)), k) * dec_ref[h]
            o = o + jnp.where(mh, _dot(att.astype(BF16), v), 0.0)
        s_old = s_ref[b]
        o = o + _dot(q, s_old.astype(BF16)) * xi_ref[...]
        kz_t = (k.astype(F32) * zeta_ref[...]).T.astype(BF16)
        s_ref[b] = jnp.where(same_head, gc_ref[...] * s_old + _dot(kz_t, v), 0.0)

        xc = o - _head_mean(o, avg)
        var = _head_mean(xc * xc, avg)
        r = xc * lax.rsqrt(var + EPS) * gn_ref[...]
        o_ref[b] = (r * gate_ref[b].astype(F32)).astype(BF16)

    @pl.when(ci == pl.num_programs(0) - 1)
    def _():
        so_ref[...] = s_ref[...]


def _ret_prompt(rq, rk, rv, gate, gn, chunk):
    bsz, length, _ = rq.shape
    rows = pl.BlockSpec((bsz, chunk, WB), lambda i: (0, i, 0))
    return pl.pallas_call(
        functools.partial(_ret_kernel, chunk=chunk, bsz=bsz),
        grid=(length // chunk,),
        in_specs=[rows, rows, rows, rows, _const_spec((1, WB))],
        out_specs=[rows, pl.BlockSpec((bsz, WB, WB), lambda i: (0, 0, 0))],
        out_shape=[jax.ShapeDtypeStruct((bsz, length, WB), BF16),
                   jax.ShapeDtypeStruct((bsz, WB, WB), F32)],
        scratch_shapes=[pltpu.VMEM((bsz, WB, WB), F32), pltpu.VMEM((HB, chunk, chunk), F32),
                        pltpu.VMEM((chunk, WB), F32), pltpu.VMEM((chunk, WB), F32), pltpu.VMEM((WB, WB), F32)],
        compiler_params=_params(("arbitrary",)),
        name="ret_prompt",
    )(rq, rk, rv, gate, gn)


_CONV_PAD = 32


def _layernorm(x, g, b):
    mu = jnp.mean(x, axis=-1, keepdims=True)
    xc = x - mu
    return xc * lax.rsqrt(jnp.mean(xc * xc, axis=-1, keepdims=True) + EPS) * g + b


_CONV_ROWS = 128


def _conv_kernel(u_ref, w_ref, b_ref, lg_ref, lb_ref, o_ref, st_ref, ext_ref, *, tm):
    i = pl.program_id(1)

    @pl.when(i == 0)
    def _():
        ext_ref[0, 0:_CONV_PAD, :] = jnp.zeros((_CONV_PAD, DC), F32)

    ext_ref[0, _CONV_PAD:_CONV_PAD + tm, :] = u_ref[0]
    span = tm + _CONV_PAD - SUBLANES
    for r in range(1, SUBLANES):
        ext_ref[r, 0:span, :] = ext_ref[0, r:r + span, :]

    base = _CONV_PAD - (CONV_W - 1)
    for rb in range(tm // _CONV_ROWS):
        acc = jnp.zeros((_CONV_ROWS, DC), F32) + b_ref[...]
        for j in range(CONV_W):
            a, r = divmod(base + j, SUBLANES)
            start = a * SUBLANES + rb * _CONV_ROWS
            acc = acc + w_ref[j:j + 1, :] * ext_ref[r, start:start + _CONV_ROWS, :]
        rows = slice(rb * _CONV_ROWS, (rb + 1) * _CONV_ROWS)
        o_ref[0, rows, :] = _silu(_layernorm(acc, lg_ref[...], lb_ref[...])).astype(BF16)

    @pl.when(i == pl.num_programs(1) - 1)
    def _():
        st_ref[0] = ext_ref[0, tm + base:tm + _CONV_PAD, :]

    ext_ref[0, 0:_CONV_PAD, :] = ext_ref[0, tm:tm + _CONV_PAD, :]


def _conv_prompt(u, w, b, lg, lb, tm):
    bsz, length, _ = u.shape
    rows = pl.BlockSpec((1, tm, DC), lambda bb, i: (bb, i, 0))
    return pl.pallas_call(
        functools.partial(_conv_kernel, tm=tm),
        grid=(bsz, length // tm),
        in_specs=[rows, _const_spec((CONV_W, DC)), _const_spec((1, DC)),
                  _const_spec((1, DC)), _const_spec((1, DC))],
        out_specs=[rows, pl.BlockSpec((1, CONV_W - 1, DC), lambda bb, i: (bb, 0, 0))],
        out_shape=[jax.ShapeDtypeStruct((bsz, length, DC), BF16),
                   jax.ShapeDtypeStruct((bsz, CONV_W - 1, DC), F32)],
        scratch_shapes=[pltpu.VMEM((SUBLANES, tm + _CONV_PAD, DC), F32)],
        compiler_params=_params(("arbitrary", "arbitrary")),
        name="conv_prompt",
    )(u, w, b, lg, lb)


_FF_CHUNK = 256
_FF_BUFS = 4
_FF_GROUP = D_FF // _FF_CHUNK


def _ffn_kernel(*refs, tm, step_mode):
    if step_mode:
        (x_ref, oa_ref, ob_ref, oc_ref, g1_ref, sh2_ref, sc2_ref, g2_ref, n_ref,
         wo_ref, wup_ref, cw_ref, cb_ref, wdn_ref, st_ref, y_ref, ns_ref, act_ref) = refs
    else:
        (x_ref, oa_ref, ob_ref, oc_ref, g1_ref, sh2_ref, sc2_ref, g2_ref, n_ref,
         wo_ref, wup_ref, cw_ref, cb_ref, wdn_ref, y_ref, ns_ref, act_ref, ctx_ref, bufs_ref) = refs
        i = pl.program_id(1)

        @pl.when(i == 0)
        def _():
            ctx_ref[...] = jnp.zeros_like(ctx_ref)

    mix = (_dot(oa_ref[0], wo_ref[0:WA, :]) + _dot(ob_ref[0], wo_ref[WA:WA + WB, :])
           + _dot(oc_ref[0], wo_ref[WA + WB:, :]))
    x1 = x_ref[0] + g1_ref[0] * (_rms(mix) * n_ref[1:2, :])
    h2 = ((_rms(x1) * n_ref[2:3, :]) * (1.0 + sc2_ref[0]) + sh2_ref[0]).astype(BF16)

    n_chunks = D_FF // _FF_CHUNK
    f = None
    for c in range(n_chunks):
        halves = []
        for half in range(2):
            c0 = half * D_FF + c * _FF_CHUNK
            cs = slice(c0, c0 + _FF_CHUNK)
            up = _dot(h2, wup_ref[:, cs])
            if step_mode:
                prev2 = st_ref[0, :, cs]
                prev1 = st_ref[0, :, 2 * D_FF + c0:2 * D_FF + c0 + _FF_CHUNK]
                ns_ref[0, :, cs] = prev1
                ns_ref[0, :, 2 * D_FF + c0:2 * D_FF + c0 + _FF_CHUNK] = up
            else:
                buf_ref = bufs_ref.at[(2 * c + half) % _FF_BUFS]
                buf_ref[0:SUBLANES, :] = ctx_ref[:, cs]
                buf_ref[SUBLANES:SUBLANES + tm, :] = up
                prev1 = buf_ref[SUBLANES - 1:SUBLANES - 1 + tm, :]
                prev2 = buf_ref[SUBLANES - 2:SUBLANES - 2 + tm, :]
                ctx_ref[:, cs] = buf_ref[tm:tm + SUBLANES, :]
            halves.append(cw_ref[0:1, cs] * prev2 + cw_ref[1:2, cs] * prev1
                          + cw_ref[2:3, cs] * up + cb_ref[:, cs])
        act_ref[:, c * _FF_CHUNK:(c + 1) * _FF_CHUNK] = (_silu(halves[0]) * halves[1]).astype(BF16)
        if (c + 1) % _FF_GROUP == 0 or c + 1 == n_chunks:
            lo = (c // _FF_GROUP) * _FF_GROUP * _FF_CHUNK
            part = _dot(act_ref[:, lo:(c + 1) * _FF_CHUNK], wdn_ref[lo:(c + 1) * _FF_CHUNK, :])
            f = part if f is None else f + part
    y_ref[0] = x1 + g2_ref[0] * (_rms(f) * n_ref[3:4, :])

    if not step_mode:
        @pl.when(i == pl.num_programs(1) - 1)
        def _():
            ns_ref[0] = ctx_ref[SUBLANES - 2:SUBLANES, :]


def _out_ffn(x, oa, ob, oc, mod, norms, wo, wup, cw, cb, wdn, tm, state=None, layer=0):
    bsz, length, d = x.shape
    step_mode = state is not None

    def rows(width):
        return pl.BlockSpec((1, tm, width), lambda b, i: (b, i, 0))

    in_specs = [rows(d), rows(WA), rows(WB), rows(DC),
                _mod_spec(mod, tm, 2), _mod_spec(mod, tm, 3), _mod_spec(mod, tm, 4), _mod_spec(mod, tm, 5),
                _const_spec((4, d)), _const_spec((d, d)), _const_spec((d, 2 * D_FF)),
                _const_spec((3, 2 * D_FF)), _const_spec((1, 2 * D_FF)), _const_spec((D_FF, d))]
    args = [x, oa, ob, oc, mod, mod, mod, mod, norms, wo, wup, cw, cb, wdn]
    if step_mode:
        in_specs.append(pl.BlockSpec((1, tm, 4 * D_FF), lambda b, i: (layer, i, 0)))
        args.append(state)
        ns_spec = pl.BlockSpec((1, tm, 4 * D_FF), lambda b, i: (b, i, 0))
        ns_shape = jax.ShapeDtypeStruct((bsz, length, 4 * D_FF), F32)
        scratch = [pltpu.VMEM((tm, D_FF), BF16)]
    else:
        ns_spec = pl.BlockSpec((1, 2, 2 * D_FF), lambda b, i: (b, 0, 0))
        ns_shape = jax.ShapeDtypeStruct((bsz, 2, 2 * D_FF), F32)
        scratch = [pltpu.VMEM((tm, D_FF), BF16), pltpu.VMEM((SUBLANES, 2 * D_FF), F32),
                   pltpu.VMEM((_FF_BUFS, tm + SUBLANES, _FF_CHUNK), F32)]
    return pl.pallas_call(
        functools.partial(_ffn_kernel, tm=tm, step_mode=step_mode),
        grid=(bsz, length // tm),
        in_specs=in_specs,
        out_specs=[rows(d), ns_spec],
        out_shape=[jax.ShapeDtypeStruct((bsz, length, d), F32), ns_shape],
        scratch_shapes=scratch,
        compiler_params=_params(("arbitrary", "arbitrary")),
        name="out_ffn_step" if step_mode else "out_ffn_seq",
    )(*args)


_SEQ_PER_STEP = 2


def _paged_kernel(pt_ref, q_ref, kn_ref, vn_ref, lfn_ref, *rest, n_pages, n_seq):
    n_blk = n_seq * n_pages
    kp = rest[0:n_blk]
    vp = rest[n_blk:2 * n_blk]
    lp = rest[2 * n_blk:3 * n_blk]
    o_ref = rest[3 * n_blk]
    pw = PAGE_SIZE * HA
    total = n_pages * pw
    past = n_pages * PAGE_SIZE

    rowh = lax.broadcasted_iota(jnp.int32, (SUBLANES, total), 0)
    lane = lax.broadcasted_iota(jnp.int32, (SUBLANES, total), 1)
    own = (lane % HA) == rowh
    pos = lax.broadcasted_iota(jnp.int32, (SUBLANES, past), 1)
    e_row = lax.broadcasted_iota(jnp.int32, (PAGE_SIZE, pw), 0)
    e_col = lax.broadcasted_iota(jnp.int32, (PAGE_SIZE, pw), 1)
    spread = jnp.where(e_col // HA == e_row, 1.0, 0.0).astype(BF16)

    for sq in range(n_seq):
        pages = range(sq * n_pages, (sq + 1) * n_pages)
        q = q_ref[sq]
        qb = q.astype(BF16)

        lf = jnp.concatenate([lp[j][0, 0] for j in pages], axis=1)
        lf = jnp.concatenate([lf, jnp.zeros_like(lf)], axis=0)
        suf = lf
        s = 1
        while s < past:
            suf = suf + jnp.where(pos + s < past, pltpu.roll(suf, past - s, 1), 0.0)
            s *= 2
        bias_hm = (suf - lf + lfn_ref[sq]) * LOG2E

        stacked = jnp.concatenate(
            [bias_hm[:, j * PAGE_SIZE:(j + 1) * PAGE_SIZE] for j in range(n_pages)], axis=0)
        hi = stacked.astype(BF16)
        r1 = stacked - hi.astype(F32)
        mid = r1.astype(BF16)
        lo = (r1 - mid.astype(F32)).astype(BF16)
        bias = _dot(hi, spread) + _dot(mid, spread) + _dot(lo, spread)

        logits = jnp.concatenate(
            [_dot_nt(qb, kp[j][0, 0].astype(BF16)) + bias[t * SUBLANES:(t + 1) * SUBLANES, :]
             for t, j in enumerate(pages)], axis=1)
        logits = jnp.where(own, logits, NEG)
        s_new = jnp.sum(q * kn_ref[sq], axis=-1, keepdims=True)
        m = jnp.maximum(jnp.max(logits, axis=-1, keepdims=True), s_new)
        p = jnp.exp2(logits - m)
        p_new = jnp.exp2(s_new - m)
        l = jnp.sum(p, axis=-1, keepdims=True) + p_new
        pb = p.astype(BF16)
        acc = p_new * vn_ref[sq]
        for t, j in enumerate(pages):
            acc = acc + _dot(pb[:, t * pw:(t + 1) * pw], vp[j][0, 0].astype(BF16))
        o_ref[sq] = acc / l


def _fox_sample(page_table, q, k_new, v_new, lf_new, pool_k, pool_v, pool_lf, layer):
    n, n_pages = page_table.shape
    pt = page_table.reshape(-1)
    pw = PAGE_SIZE * HA
    n_seq = _SEQ_PER_STEP if n % _SEQ_PER_STEP == 0 else 1
    row = pl.BlockSpec((n_seq, SUBLANES, DA), lambda i, pt: (i, 0, 0))

    def pages(width_shape):
        return [pl.BlockSpec((1, 1) + width_shape,
                             lambda i, pt, sq=sq, j=j: (layer, pt[(i * n_seq + sq) * n_pages + j], 0, 0))
                for sq in range(n_seq) for j in range(n_pages)]

    in_specs = ([row, row, row, pl.BlockSpec((n_seq, SUBLANES, 1), lambda i, pt: (i, 0, 0))]
                + pages((pw, DA)) + pages((pw, DA)) + pages((HA, PAGE_SIZE)))
    grid_spec = pltpu.PrefetchScalarGridSpec(
        num_scalar_prefetch=1, grid=(n // n_seq,), in_specs=in_specs,
        out_specs=pl.BlockSpec((n_seq, SUBLANES, DA), lambda i, pt: (i, 0, 0)))
    n_blk = n_seq * n_pages
    return pl.pallas_call(
        functools.partial(_paged_kernel, n_pages=n_pages, n_seq=n_seq),
        grid_spec=grid_spec,
        out_shape=jax.ShapeDtypeStruct((n, SUBLANES, DA), F32),
        compiler_params=_params(("arbitrary",)),
        name="fox_sample",
    )(pt, q, k_new, v_new, lf_new, *([pool_k] * n_blk), *([pool_v] * n_blk), *([pool_lf] * n_blk))


def _sret_kernel(q_ref, k_ref, v_ref, gate_ref, gn_ref, s0_ref, o_ref, sn_ref):
    q = q_ref[0]
    k = k_ref[0]
    v = v_ref[0]
    hw = DKB * DVB
    lane = lax.broadcasted_iota(jnp.int32, (DKB, hw), 1)
    rowi = lax.broadcasted_iota(jnp.int32, (DKB, hw), 0)
    rep = (lane // DVB == rowi).astype(BF16)
    til = (lane % DVB == rowi).astype(BF16)
    outs = []
    for h in range(HB):
        sl = slice(h * DKB, (h + 1) * DKB)
        g = math.exp(_LOG_G[h])
        sn = g * s0_ref[0, :, h * hw:(h + 1) * hw] + _dot(k[:, sl], rep) * _dot(v[:, sl], til)
        sn_ref[0, :, h * hw:(h + 1) * hw] = sn
        prod = _dot(q[:, sl], rep) * sn
        acc = prod[:, 0:LANES]
        for j in range(1, hw // LANES):
            acc = acc + prod[:, j * LANES:(j + 1) * LANES]
        oh = acc[:, 0:DVB] + acc[:, DVB:LANES]
        mu = jnp.mean(oh, axis=-1, keepdims=True)
        xc = oh - mu
        outs.append(xc * lax.rsqrt(jnp.mean(xc * xc, axis=-1, keepdims=True) + EPS))
    r = jnp.concatenate(outs, axis=1) * gn_ref[...]
    o_ref[0] = (r * gate_ref[0].astype(F32)).astype(BF16)


def _ret_sample(rq, rk, rv, gate, gn, state, layer, nb):
    _, n, _ = rq.shape
    sw = HB * DKB * DVB
    rows = pl.BlockSpec((1, nb, WB), lambda i: (0, i, 0))
    return pl.pallas_call(
        _sret_kernel,
        grid=(n // nb,),
        in_specs=[rows, rows, rows, rows, _const_spec((1, WB)),
                  pl.BlockSpec((1, nb, sw), lambda i: (layer, i, 0))],
        out_specs=[rows, pl.BlockSpec((1, nb, sw), lambda i: (0, i, 0))],
        out_shape=[jax.ShapeDtypeStruct((1, n, WB), BF16), jax.ShapeDtypeStruct((1, n, sw), F32)],
        compiler_params=_params(("arbitrary",)),
        name="ret_sample",
    )(rq, rk, rv, gate, gn, state)


def _sconv_kernel(ctx_ref, u_ref, w_ref, b_ref, lg_ref, lb_ref, o_ref, ns_ref):
    ctx = ctx_ref[0]
    u = u_ref[...]
    w = w_ref[...]
    cv = (jnp.sum(ctx * w[0:CONV_W - 1, :][None], axis=1, keepdims=True)
          + u * w[CONV_W - 1:CONV_W, :][None] + b_ref[...][None])
    o_ref[...] = _silu(_layernorm(cv, lg_ref[...][None], lb_ref[...][None])).astype(BF16)
    ns_ref[0, :, 0:CONV_W - 2, :] = ctx[:, 1:CONV_W - 1, :]
    ns_ref[0, :, CONV_W - 2:CONV_W - 1, :] = u


def _conv_sample(state, u, w, b, lg, lb, layer, nb):
    n = u.shape[0]
    return pl.pallas_call(
        _sconv_kernel,
        grid=(n // nb,),
        in_specs=[pl.BlockSpec((1, nb, CONV_W - 1, DC), lambda i: (layer, i, 0, 0)),
                  pl.BlockSpec((nb, 1, DC), lambda i: (i, 0, 0)),
                  _const_spec((CONV_W, DC)), _const_spec((1, DC)), _const_spec((1, DC)), _const_spec((1, DC))],
        out_specs=[pl.BlockSpec((nb, 1, DC), lambda i: (i, 0, 0)),
                   pl.BlockSpec((1, nb, CONV_W - 1, DC), lambda i: (0, i, 0, 0))],
        out_shape=[jax.ShapeDtypeStruct((n, 1, DC), BF16),
                   jax.ShapeDtypeStruct((1, n, CONV_W - 1, DC), F32)],
        compiler_params=_params(("arbitrary",)),
        name="conv_sample",
    )(state, u, w, b, lg, lb)


def _reorder_w_in(w):
    o = 3 * WA
    main = jnp.concatenate([w[:, :o], w[:, o + HA:]], axis=1)
    tail = jnp.pad(w[:, o:o + HA], ((0, 0), (0, LANES - HA)))
    return jnp.concatenate([main, tail], axis=1).astype(BF16)


def _pick(total, pref):
    t = min(total, pref)
    assert total % t == 0
    return t


def kernel(x_prompt, x_sample, cache_k, cache_v, cache_logf, state_ret, state_conv, state_ffn_conv, page_table,
           c_prompt, c_sample, ada_w, ada_b, norms, w_in, b_f, ret_gn_g, conv_w, conv_b, conv_ln_g, conv_ln_b,
           w_o, ffn_up, ffn_conv_w, ffn_conv_b, ffn_down):
    depth = ada_w.shape[0]
    bp, sp, d = x_prompt.shape
    ns, ts, _ = x_sample.shape
    assert ts == 1 and d == D_MODEL
    n_pool = cache_k.shape[1]
    n_pages = page_table.shape[1]

    n_c = ns + bp
    n_c_pad = -(-n_c // SUBLANES) * SUBLANES
    c_all = jnp.pad(jnp.concatenate([c_sample, c_prompt], axis=0), ((0, n_c_pad - n_c), (0, 0)))
    mod_all = _ada_mod(c_all, ada_w, ada_b)

    cos_p, sin_p = _rope_tables(sp, 0, 1)
    cos_s, sin_s = _rope_tables(ns, n_pages * PAGE_SIZE, 0)

    pool_k = cache_k.reshape(depth, n_pool, PAGE_SIZE * HA, DA)
    pool_v = cache_v.reshape(depth, n_pool, PAGE_SIZE * HA, DA)
    pool_lf = cache_logf.transpose(0, 1, 3, 2)
    st_ret = state_ret.reshape(depth, ns, HB * DKB * DVB)
    st_ffn = state_ffn_conv.reshape(depth, ns, 4 * D_FF)

    tm_p = _pick(sp, 512)
    tq = _pick(sp, 512)
    chunk = _pick(sp, 128)
    nb_s = _pick(ns, 32)

    xp = x_prompt
    xs = x_sample.reshape(1, ns, d)
    p_out, s_out = [], []
    kv_p = kv_s = None

    def heads_on_rows(a):
        return jnp.pad(a.reshape(ns, HA, DA), ((0, 0), (0, SUBLANES - HA), (0, 0)))

    for l in range(depth):
        w_re = _reorder_w_in(w_in[l])
        bf_pad = jnp.pad(b_f[l], (0, LANES - HA)).reshape(1, LANES)
        wo = w_o[l].astype(BF16)
        wup = ffn_up[l].astype(BF16)
        wdn = ffn_down[l].astype(BF16)
        n0 = norms[l, 0:1]
        gn = ret_gn_g[l].reshape(1, WB)
        cb = conv_b[l].reshape(1, DC)
        clg = conv_ln_g[l].reshape(1, DC)
        clb = conv_ln_b[l].reshape(1, DC)
        fcb = ffn_conv_b[l].reshape(1, 2 * D_FF)
        mod_s = mod_all[l, 0:ns].reshape(1, ns, 6 * d)
        mod_p = mod_all[l, ns:ns + bp].reshape(bp, 1, 6 * d)

        ko, vo, qa, ka, va, lft, rq, rk, rv, gate, u = _in_proj(
            xp, mod_p, n0, w_re, bf_pad, cos_p, sin_p, tm_p, depth, l, kv_p)
        kv_p = (ko, vo)
        oa = _fox_prompt(qa, ka, va, _cumsum(lft), tq)
        ob, s_bd = _ret_prompt(rq, rk, rv, gate, gn, chunk)
        oc, conv_st = _conv_prompt(u, conv_w[l], cb, clg, clb, tm_p)
        xp, ffn_st = _out_ffn(xp, oa, ob, oc, mod_p, norms[l], wo, wup, ffn_conv_w[l], fcb, wdn, tm_p)
        s5 = s_bd.reshape(bp, HB, DKB, HB, DVB)
        ret_st = jnp.stack([s5[:, h, :, h, :] for h in range(HB)], axis=1)
        p_out.append((lft[:, 0:HA, :].transpose(0, 2, 1), ret_st, conv_st, ffn_st))

        ko, vo, qa, ka, va, lft, rq, rk, rv, gate, u = _in_proj(
            xs, mod_s, n0, w_re, bf_pad, cos_s, sin_s, ns, depth, l, kv_s)
        kv_s = (ko, vo)
        lfn = lft[0].T.reshape(ns, SUBLANES, 1)
        oa = _fox_sample(page_table, heads_on_rows(qa.astype(F32)), heads_on_rows(ko[l, 0]),
                         heads_on_rows(vo[l, 0]), lfn, pool_k, pool_v, pool_lf, l)
        oa = oa[:, 0:HA, :].reshape(1, ns, WA).astype(BF16)
        ob, ret_new = _ret_sample(rq, rk, rv, gate, gn, st_ret, l, nb_s)
        oc, conv_new = _conv_sample(state_conv, u.reshape(ns, 1, DC), conv_w[l], cb, clg, clb, l, nb_s)
        xs, ffn_new = _out_ffn(xs, oa, ob, oc.reshape(1, ns, DC), mod_s, norms[l],
                               wo, wup, ffn_conv_w[l], fcb, wdn, ns, state=st_ffn, layer=l)
        s_out.append((lft[0, 0:HA, :].T.reshape(ns, 1, HA), ret_new.reshape(ns, HB, DKB, DVB),
                      conv_new[0], ffn_new.reshape(ns, 2, 2 * D_FF)))

    plf, pret, pconv, pffn = [jnp.stack(a) for a in zip(*p_out)]
    slf, sret, sconv, sffn = [jnp.stack(a) for a in zip(*s_out)]
    pk, pv = [a.reshape(depth, bp, sp, HA, DA) for a in kv_p]
    sk, sv = [a.reshape(depth, ns, 1, HA, DA) for a in kv_s]
    return (xp, xs.reshape(ns, 1, d), pk, pv, plf, pret, pconv, pffn, sk, sv, slf, sret, sconv, sffn)
```

```python
import functools
import math

import jax
import jax.numpy as jnp
import numpy as np
from jax import lax
from jax.experimental import pallas as pl
from jax.experimental.pallas import tpu as pltpu

F32 = jnp.float32
BF16 = jnp.bfloat16

D_MODEL = 1024
HA, DA = 4, 128
WA = HA * DA
HB, DKB, DVB = 4, 64, 64
WB = HB * DVB
DC = 256
CONV_W = 31
D_FF = 2816
PAGE_SIZE = 128
ROPE_BASE = 10000.0
EPS = 1e-6
NEG = -1e30
LOG2E = math.log2(math.e)

LANES = 128
SUBLANES = 8
VMEM_LIMIT = 56 * 1024 * 1024

_OQ, _OK, _OV = 0, 512, 1024
_OBQ, _OBK, _OBV, _OBG = 1536, 1792, 2048, 2304
_OCA, _OCB, _OAF = 2560, 2816, 3072
N_IN_PAD = 3200

_LOG_G = [math.log1p(-(2.0 ** (-5.0 - h))) for h in range(HB)]


def _sigmoid(x):
    return 1.0 / (1.0 + jnp.exp(-x))


def _silu(x):
    return x * _sigmoid(x)


def _log_sigmoid(x):
    return jnp.minimum(x, 0.0) - jnp.log1p(jnp.exp(-jnp.abs(x)))


def _rms(x):
    return x * lax.rsqrt(jnp.mean(x * x, axis=-1, keepdims=True) + EPS)


def _dot(a, b):
    return jnp.dot(a, b, preferred_element_type=F32)


def _dot_nt(a, b):
    return lax.dot_general(a, b, (((1,), (1,)), ((), ())), preferred_element_type=F32)


def _params(sem):
    return pltpu.CompilerParams(dimension_semantics=sem, vmem_limit_bytes=VMEM_LIMIT)


def _const_spec(shape):
    nd = len(shape)
    return pl.BlockSpec(shape, lambda *_: (0,) * nd, pipeline_mode=pl.Buffered(1))


def _layer_spec(shape, layer):
    nd = len(shape)
    return pl.BlockSpec((1,) + tuple(shape), lambda *_: (layer,) + (0,) * nd, pipeline_mode=pl.Buffered(1))


def _ada_kernel(c_ref, w_ref, b_ref, o_ref):
    s = _silu(c_ref[...]).astype(BF16)
    o_ref[0] = _dot(s, w_ref[0].astype(BF16)) + b_ref[0]


def _ada_mod(c_all, ada_w, ada_b):
    depth, d, n6 = ada_w.shape
    r = c_all.shape[0]
    tn = 1536
    return pl.pallas_call(
        _ada_kernel,
        grid=(depth, n6 // tn),
        in_specs=[pl.BlockSpec((r, d), lambda l, j: (0, 0)),
                  pl.BlockSpec((1, d, tn), lambda l, j: (l, 0, j)),
                  pl.BlockSpec((1, 1, tn), lambda l, j: (l, 0, j))],
        out_specs=pl.BlockSpec((1, r, tn), lambda l, j: (l, 0, j)),
        out_shape=jax.ShapeDtypeStruct((depth, r, n6), F32),
        compiler_params=_params(("arbitrary", "arbitrary")),
        name="ada_mod",
    )(c_all, ada_w, ada_b.reshape(depth, 1, n6))


def _rope_kernel(inv_ref, cos_ref, sin_ref, *, pos0, stride, tm):
    i = pl.program_id(0)
    row = lax.broadcasted_iota(jnp.int32, (tm, LANES), 0) + i * tm
    lane = lax.broadcasted_iota(jnp.int32, (tm, LANES), 1)
    ang = (pos0 + stride * row).astype(F32) * inv_ref[...]
    cos_ref[...] = jnp.cos(ang)
    s = jnp.sin(ang)
    sin_ref[...] = jnp.where((lane % DKB) < DKB // 2, -s, s)


def _rope_tables(length, pos0, stride):
    half = DKB // 2
    inv = ROPE_BASE ** (-jnp.arange(half, dtype=F32) / half)
    inv = jnp.tile(inv, LANES // half).reshape(1, LANES)
    tm = min(length, 1024)
    out = jax.ShapeDtypeStruct((length, LANES), F32)
    return pl.pallas_call(
        functools.partial(_rope_kernel, pos0=pos0, stride=stride, tm=tm),
        grid=(length // tm,),
        in_specs=[pl.BlockSpec((1, LANES), lambda i: (0, 0))],
        out_specs=[pl.BlockSpec((tm, LANES), lambda i: (i, 0))] * 2,
        out_shape=[out, out],
        compiler_params=_params(("arbitrary",)),
        name="rope_tables",
    )(inv)


def _in_kernel(*refs, tm, aliased):
    if aliased:
        refs = refs[:8] + refs[10:]
    (x_ref, sh_ref, sc_ref, g_ref, w_ref, bf_ref, cos_ref, sin_ref,
     ko_ref, vo_ref, qa_ref, ka_ref, va_ref, lft_ref, rq_ref, rk_ref, rv_ref, gate_ref, u_ref) = refs
    x = x_ref[0]
    h = _rms(x) * g_ref[...]
    hb = (h * (1.0 + sc_ref[0]) + sh_ref[0]).astype(BF16)

    def proj(off, width):
        return _dot_nt(hb, w_ref[off:off + width, :])

    qa_ref[0] = (proj(_OQ, WA) * (DA ** -0.5 * LOG2E)).astype(BF16)
    k = proj(_OK, WA)
    ka_ref[0] = k.astype(BF16)
    v = proj(_OV, WA)
    va_ref[0] = v.astype(BF16)
    for hd in range(HA):
        ko_ref[0, 0, pl.ds(hd, tm, stride=HA), :] = k[:, hd * DA:(hd + 1) * DA]
        vo_ref[0, 0, pl.ds(hd, tm, stride=HA), :] = v[:, hd * DA:(hd + 1) * DA]

    lf = _log_sigmoid(proj(_OAF, LANES) + bf_ref[...])
    lft_ref[0] = lf.T[0:SUBLANES, :]

    cos = cos_ref[...]
    sin = sin_ref[...]
    lane = lax.broadcasted_iota(jnp.int32, cos.shape, 1)
    first = (lane % DKB) < DKB // 2

    def rope(t):
        partner = jnp.where(first, pltpu.roll(t, LANES - DKB // 2, 1), pltpu.roll(t, DKB // 2, 1))
        return t * cos + partner * sin

    bq = proj(_OBQ, WB)
    bk = proj(_OBK, WB)
    for j in range(WB // LANES):
        sl = slice(j * LANES, (j + 1) * LANES)
        rq_ref[0, :, sl] = rope(bq[:, sl]).astype(BF16)
        rk_ref[0, :, sl] = (rope(bk[:, sl]) * (DKB ** -0.5)).astype(BF16)
    rv_ref[0] = proj(_OBV, WB).astype(BF16)
    gate_ref[0] = _silu(proj(_OBG, WB)).astype(BF16)
    u_ref[0] = proj(_OCA, DC) * _sigmoid(proj(_OCB, DC))


def _mod_spec(mod, tm, col):
    rows = mod.shape[1]
    if rows == 1:
        return pl.BlockSpec((1, 1, D_MODEL), lambda b, i: (b, 0, col))
    return pl.BlockSpec((1, tm, D_MODEL), lambda b, i: (b, i, col))


def _in_proj(x, mod, norm_g, w_re, bf_pad, cos, sin, tm, depth, layer, kv_prev):
    bsz, length, d = x.shape
    grid = (bsz, length // tm)

    def rows(width):
        return pl.BlockSpec((1, tm, width), lambda b, i: (b, i, 0))

    def shp(width, dt):
        return jax.ShapeDtypeStruct((bsz, length, width), dt)

    tab = pl.BlockSpec((tm, LANES), lambda b, i: (i, 0))
    cache_spec = pl.BlockSpec((1, 1, tm * HA, DA), lambda b, i: (layer, b, i, 0))
    cache_shape = jax.ShapeDtypeStruct((depth, bsz, length * HA, DA), F32)
    in_specs = [rows(d), _mod_spec(mod, tm, 0), _mod_spec(mod, tm, 1),
                _const_spec((1, d)), _const_spec((N_IN_PAD, d)), _const_spec((1, LANES)), tab, tab]
    args = [x, mod, mod, norm_g, w_re, bf_pad, cos, sin]
    aliases = {}
    if kv_prev is not None:
        in_specs += [pl.BlockSpec(memory_space=pl.ANY)] * 2
        args += list(kv_prev)
        aliases = {8: 0, 9: 1}
    return pl.pallas_call(
        functools.partial(_in_kernel, tm=tm, aliased=kv_prev is not None),
        grid=grid,
        in_specs=in_specs,
        out_specs=[cache_spec, cache_spec, rows(WA), rows(WA), rows(WA),
                   pl.BlockSpec((1, SUBLANES, tm), lambda b, i: (b, 0, i)),
                   rows(WB), rows(WB), rows(WB), rows(WB), rows(DC)],
        out_shape=[cache_shape, cache_shape, shp(WA, BF16), shp(WA, BF16), shp(WA, BF16),
                   jax.ShapeDtypeStruct((bsz, SUBLANES, length), F32),
                   shp(WB, BF16), shp(WB, BF16), shp(WB, BF16), shp(WB, BF16), shp(DC, F32)],
        input_output_aliases=aliases,
        compiler_params=_params(("arbitrary", "arbitrary")),
        name="in_proj",
    )(*args)


def _cumsum_kernel(x_ref, o_ref):
    x = x_ref[0]
    length = x.shape[1]
    lane = lax.broadcasted_iota(jnp.int32, x.shape, 1)
    s = 1
    while s < length:
        x = x + jnp.where(lane >= s, pltpu.roll(x, s, 1), 0.0)
        s *= 2
    o_ref[0] = x * LOG2E


def _cumsum(lft):
    bsz, r, length = lft.shape
    return pl.pallas_call(
        _cumsum_kernel,
        grid=(bsz,),
        in_specs=[pl.BlockSpec((1, r, length), lambda b: (b, 0, 0))],
        out_specs=pl.BlockSpec((1, r, length), lambda b: (b, 0, 0)),
        out_shape=jax.ShapeDtypeStruct(lft.shape, F32),
        compiler_params=_params(("arbitrary",)),
        name="logf_cumsum",
    )(lft)


def _fox_kernel(q_ref, k_ref, v_ref, ck_ref, o_ref, *, tq):
    qi = pl.program_id(1)
    row = lax.broadcasted_iota(jnp.int32, (tq, tq), 0)
    col = lax.broadcasted_iota(jnp.int32, (tq, tq), 1)
    causal = row >= col
    ones = jnp.ones((tq, DA), BF16)

    def step(kb, carry, masked):
        off = pl.multiple_of(kb * tq, tq)
        out = []
        for h in range(HA):
            m, acc = carry[h]
            sl = slice(h * DA, (h + 1) * DA)
            k = k_ref[0, pl.ds(off, tq), sl]
            v1 = jnp.concatenate([v_ref[0, pl.ds(off, tq), sl], ones], axis=1)
            s = _dot_nt(q_ref[0, :, sl], k) - ck_ref[0, kb, h:h + 1, :]
            if masked:
                s = jnp.where(causal, s, NEG)
            m_new = jnp.maximum(m, jnp.max(s, axis=-1, keepdims=True))
            p = jnp.exp2(s - m_new)
            acc = jnp.exp2(m - m_new) * acc + _dot(p.astype(BF16), v1)
            out.append((m_new, acc))
        return tuple(out)

    init = tuple((jnp.full((tq, 1), NEG, F32), jnp.zeros((tq, 2 * DA), F32)) for _ in range(HA))
    carry = lax.fori_loop(0, qi, functools.partial(step, masked=False), init)
    carry = step(qi, carry, True)
    for h in range(HA):
        acc = carry[h][1]
        o_ref[0, :, h * DA:(h + 1) * DA] = (acc[:, 0:DA] / acc[:, DA:2 * DA]).astype(BF16)


def _fox_prompt(qa, ka, va, cum, tq):
    bsz, length, _ = qa.shape
    nb = length // tq
    ck = cum.reshape(bsz, SUBLANES, nb, tq).transpose(0, 2, 1, 3)
    full = pl.BlockSpec((1, length, WA), lambda b, i: (b, 0, 0), pipeline_mode=pl.Buffered(1))
    return pl.pallas_call(
        functools.partial(_fox_kernel, tq=tq),
        grid=(bsz, nb),
        in_specs=[pl.BlockSpec((1, tq, WA), lambda b, i: (b, i, 0)), full, full,
                  pl.BlockSpec((1, nb, SUBLANES, tq), lambda b, i: (b, 0, 0, 0))],
        out_specs=pl.BlockSpec((1, tq, WA), lambda b, i: (b, i, 0)),
        out_shape=jax.ShapeDtypeStruct((bsz, length, WA), BF16),
        compiler_params=_params(("arbitrary", "arbitrary")),
        name="fox_prompt",
    )(qa, ka, va, ck)


def _head_mean(z, avg):
    hi = z.astype(BF16)
    lo = (z - hi.astype(F32)).astype(BF16)
    return _dot(hi, avg) + _dot(lo, avg)


def _group_avg_matrix():
    r = lax.broadcasted_iota(jnp.int32, (WB, WB), 0) // DVB
    c = lax.broadcasted_iota(jnp.int32, (WB, WB), 1) // DVB
    return r == c


def _lane_log_g(shape, axis):
    head = lax.broadcasted_iota(jnp.int32, shape, axis) // DVB
    lg = jnp.full(shape, _LOG_G[HB - 1], F32)
    for h in range(HB - 1):
        lg = jnp.where(head == h, _LOG_G[h], lg)
    return lg


def _ret_kernel(q_ref, k_ref, v_ref, gate_ref, gn_ref, o_ref, so_ref,
                s_ref, dec_ref, xi_ref, zeta_ref, gc_ref, *, chunk, bsz):
    ci = pl.program_id(0)
    same_head = _group_avg_matrix()

    @pl.when(ci == 0)
    def _():
        s_ref[...] = jnp.zeros_like(s_ref)
        r_i = lax.broadcasted_iota(jnp.int32, (chunk, chunk), 0)
        c_i = lax.broadcasted_iota(jnp.int32, (chunk, chunk), 1)
        rel = (r_i - c_i).astype(F32)
        for h in range(HB):
            dec_ref[h] = jnp.where(rel >= 0, jnp.exp(_LOG_G[h] * jnp.maximum(rel, 0.0)), 0.0)
        lg = _lane_log_g((chunk, WB), 1)
        idx = lax.broadcasted_iota(jnp.int32, (chunk, WB), 0).astype(F32)
        xi_ref[...] = jnp.exp(lg * (idx + 1.0))
        zeta_ref[...] = jnp.exp(lg * (chunk - 1.0 - idx))
        gc_ref[...] = jnp.exp(_lane_log_g((WB, WB), 0) * float(chunk))

    head = lax.broadcasted_iota(jnp.int32, (1, WB), 1) // DVB
    avg = jnp.where(same_head, 1.0 / DVB, 0.0).astype(BF16)
    for b in range(bsz):
        q = q_ref[b]
        k = k_ref[b]
        v = v_ref[b]
        o = jnp.zeros((chunk, WB), F32)
        for h in range(HB):
            mh = head == h
            att = _dot_nt(jnp.where(mh, q, jnp.zeros_like(q)), k) * dec_ref[h]
            o = o + jnp.where(mh, _dot(att.astype(BF16), v), 0.0)
        s_old = s_ref[b]
        o = o + _dot(q, s_old.astype(BF16)) * xi_ref[...]
        kz_t = (k.astype(F32) * zeta_ref[...]).T.astype(BF16)
        s_ref[b] = jnp.where(same_head, gc_ref[...] * s_old + _dot(kz_t, v), 0.0)

        xc = o - _head_mean(o, avg)
        var = _head_mean(xc * xc, avg)
        r = xc * lax.rsqrt(var + EPS) * gn_ref[...]
        o_ref[b] = (r * gate_ref[b].astype(F32)).astype(BF16)

    @pl.when(ci == pl.num_programs(0) - 1)
    def _():
        so_ref[...] = s_ref[...]


def _ret_prompt(rq, rk, rv, gate, gn, chunk):
    bsz, length, _ = rq.shape
    rows = pl.BlockSpec((bsz, chunk, WB), lambda i: (0, i, 0))
    return pl.pallas_call(
        functools.partial(_ret_kernel, chunk=chunk, bsz=bsz),
        grid=(length // chunk,),
        in_specs=[rows, rows, rows, rows, _const_spec((1, WB))],
        out_specs=[rows, pl.BlockSpec((bsz, WB, WB), lambda i: (0, 0, 0))],
        out_shape=[jax.ShapeDtypeStruct((bsz, length, WB), BF16),
                   jax.ShapeDtypeStruct((bsz, WB, WB), F32)],
        scratch_shapes=[pltpu.VMEM((bsz, WB, WB), F32), pltpu.VMEM((HB, chunk, chunk), F32),
                        pltpu.VMEM((chunk, WB), F32), pltpu.VMEM((chunk, WB), F32), pltpu.VMEM((WB, WB), F32)],
        compiler_params=_params(("arbitrary",)),
        name="ret_prompt",
    )(rq, rk, rv, gate, gn)


_CONV_PAD = 32


def _layernorm(x, g, b):
    mu = jnp.mean(x, axis=-1, keepdims=True)
    xc = x - mu
    return xc * lax.rsqrt(jnp.mean(xc * xc, axis=-1, keepdims=True) + EPS) * g + b


_CONV_ROWS = 128


def _conv_kernel(u_ref, w_ref, b_ref, lg_ref, lb_ref, o_ref, st_ref, ext_ref, *, tm):
    i = pl.program_id(1)

    @pl.when(i == 0)
    def _():
        ext_ref[0, 0:_CONV_PAD, :] = jnp.zeros((_CONV_PAD, DC), F32)

    ext_ref[0, _CONV_PAD:_CONV_PAD + tm, :] = u_ref[0]
    span = tm + _CONV_PAD - SUBLANES
    for r in range(1, SUBLANES):
        ext_ref[r, 0:span, :] = ext_ref[0, r:r + span, :]

    base = _CONV_PAD - (CONV_W - 1)
    for rb in range(tm // _CONV_ROWS):
        acc = jnp.zeros((_CONV_ROWS, DC), F32) + b_ref[...]
        for j in range(CONV_W):
            a, r = divmod(base + j, SUBLANES)
            start = a * SUBLANES + rb * _CONV_ROWS
            acc = acc + w_ref[j:j + 1, :] * ext_ref[r, start:start + _CONV_ROWS, :]
        rows = slice(rb * _CONV_ROWS, (rb + 1) * _CONV_ROWS)
        o_ref[0, rows, :] = _silu(_layernorm(acc, lg_ref[...], lb_ref[...])).astype(BF16)

    @pl.when(i == pl.num_programs(1) - 1)
    def _():
        st_ref[0] = ext_ref[0, tm + base:tm + _CONV_PAD, :]

    ext_ref[0, 0:_CONV_PAD, :] = ext_ref[0, tm:tm + _CONV_PAD, :]


def _conv_prompt(u, w, b, lg, lb, tm):
    bsz, length, _ = u.shape
    rows = pl.BlockSpec((1, tm, DC), lambda bb, i: (bb, i, 0))
    return pl.pallas_call(
        functools.partial(_conv_kernel, tm=tm),
        grid=(bsz, length // tm),
        in_specs=[rows, _const_spec((CONV_W, DC)), _const_spec((1, DC)),
                  _const_spec((1, DC)), _const_spec((1, DC))],
        out_specs=[rows, pl.BlockSpec((1, CONV_W - 1, DC), lambda bb, i: (bb, 0, 0))],
        out_shape=[jax.ShapeDtypeStruct((bsz, length, DC), BF16),
                   jax.ShapeDtypeStruct((bsz, CONV_W - 1, DC), F32)],
        scratch_shapes=[pltpu.VMEM((SUBLANES, tm + _CONV_PAD, DC), F32)],
        compiler_params=_params(("arbitrary", "arbitrary")),
        name="conv_prompt",
    )(u, w, b, lg, lb)


_FF_CHUNK = 256
_FF_BUFS = 4
_FF_GROUP = D_FF // _FF_CHUNK


def _ffn_kernel(*refs, tm, step_mode):
    if step_mode:
        (x_ref, oa_ref, ob_ref, oc_ref, g1_ref, sh2_ref, sc2_ref, g2_ref, n_ref,
         wo_ref, wup_ref, cw_ref, cb_ref, wdn_ref, st_ref, y_ref, ns_ref, act_ref) = refs
    else:
        (x_ref, oa_ref, ob_ref, oc_ref, g1_ref, sh2_ref, sc2_ref, g2_ref, n_ref,
         wo_ref, wup_ref, cw_ref, cb_ref, wdn_ref, y_ref, ns_ref, act_ref, ctx_ref, bufs_ref) = refs
        i = pl.program_id(1)

        @pl.when(i == 0)
        def _():
            ctx_ref[...] = jnp.zeros_like(ctx_ref)

    wo_ref, wup_ref, wdn_ref = wo_ref.at[0], wup_ref.at[0], wdn_ref.at[0]
    mix = (_dot(oa_ref[0], wo_ref[0:WA, :]) + _dot(ob_ref[0], wo_ref[WA:WA + WB, :])
           + _dot(oc_ref[0], wo_ref[WA + WB:, :]))
    x1 = x_ref[0] + g1_ref[0] * (_rms(mix) * n_ref[1:2, :])
    h2 = ((_rms(x1) * n_ref[2:3, :]) * (1.0 + sc2_ref[0]) + sh2_ref[0]).astype(BF16)

    n_chunks = D_FF // _FF_CHUNK
    f = None
    for c in range(n_chunks):
        halves = []
        for half in range(2):
            c0 = half * D_FF + c * _FF_CHUNK
            cs = slice(c0, c0 + _FF_CHUNK)
            up = _dot(h2, wup_ref[:, cs])
            if step_mode:
                prev2 = st_ref[0, :, cs]
                prev1 = st_ref[0, :, 2 * D_FF + c0:2 * D_FF + c0 + _FF_CHUNK]
                ns_ref[0, :, cs] = prev1
                ns_ref[0, :, 2 * D_FF + c0:2 * D_FF + c0 + _FF_CHUNK] = up
            else:
                buf_ref = bufs_ref.at[(2 * c + half) % _FF_BUFS]
                buf_ref[0:SUBLANES, :] = ctx_ref[:, cs]
                buf_ref[SUBLANES:SUBLANES + tm, :] = up
                prev1 = buf_ref[SUBLANES - 1:SUBLANES - 1 + tm, :]
                prev2 = buf_ref[SUBLANES - 2:SUBLANES - 2 + tm, :]
                ctx_ref[:, cs] = buf_ref[tm:tm + SUBLANES, :]
            halves.append(cw_ref[0:1, cs] * prev2 + cw_ref[1:2, cs] * prev1
                          + cw_ref[2:3, cs] * up + cb_ref[:, cs])
        act_ref[:, c * _FF_CHUNK:(c + 1) * _FF_CHUNK] = (_silu(halves[0]) * halves[1]).astype(BF16)
        if (c + 1) % _FF_GROUP == 0 or c + 1 == n_chunks:
            lo = (c // _FF_GROUP) * _FF_GROUP * _FF_CHUNK
            part = _dot(act_ref[:, lo:(c + 1) * _FF_CHUNK], wdn_ref[lo:(c + 1) * _FF_CHUNK, :])
            f = part if f is None else f + part
    y_ref[0] = x1 + g2_ref[0] * (_rms(f) * n_ref[3:4, :])

    if not step_mode:
        @pl.when(i == pl.num_programs(1) - 1)
        def _():
            ns_ref[0] = ctx_ref[SUBLANES - 2:SUBLANES, :]


def _out_ffn(x, oa, ob, oc, mod, norms, wo, wup, cw, cb, wdn, tm, state=None, layer=0):
    bsz, length, d = x.shape
    step_mode = state is not None

    def rows(width):
        return pl.BlockSpec((1, tm, width), lambda b, i: (b, i, 0))

    in_specs = [rows(d), rows(WA), rows(WB), rows(DC),
                _mod_spec(mod, tm, 2), _mod_spec(mod, tm, 3), _mod_spec(mod, tm, 4), _mod_spec(mod, tm, 5),
                _const_spec((4, d)), _layer_spec((d, d), layer), _layer_spec((d, 2 * D_FF), layer),
                _const_spec((3, 2 * D_FF)), _const_spec((1, 2 * D_FF)), _layer_spec((D_FF, d), layer)]
    args = [x, oa, ob, oc, mod, mod, mod, mod, norms, wo, wup, cw, cb, wdn]
    if step_mode:
        in_specs.append(pl.BlockSpec((1, tm, 4 * D_FF), lambda b, i: (layer, i, 0)))
        args.append(state)
        ns_spec = pl.BlockSpec((1, tm, 4 * D_FF), lambda b, i: (b, i, 0))
        ns_shape = jax.ShapeDtypeStruct((bsz, length, 4 * D_FF), F32)
        scratch = [pltpu.VMEM((tm, D_FF), BF16)]
    else:
        ns_spec = pl.BlockSpec((1, 2, 2 * D_FF), lambda b, i: (b, 0, 0))
        ns_shape = jax.ShapeDtypeStruct((bsz, 2, 2 * D_FF), F32)
        scratch = [pltpu.VMEM((tm, D_FF), BF16), pltpu.VMEM((SUBLANES, 2 * D_FF), F32),
                   pltpu.VMEM((_FF_BUFS, tm + SUBLANES, _FF_CHUNK), F32)]
    return pl.pallas_call(
        functools.partial(_ffn_kernel, tm=tm, step_mode=step_mode),
        grid=(bsz, length // tm),
        in_specs=in_specs,
        out_specs=[rows(d), ns_spec],
        out_shape=[jax.ShapeDtypeStruct((bsz, length, d), F32), ns_shape],
        scratch_shapes=scratch,
        compiler_params=_params(("arbitrary", "arbitrary")),
        name="out_ffn_step" if step_mode else "out_ffn_seq",
    )(*args)


_SEQ_PER_STEP = 2


def _paged_kernel(pt_ref, q_ref, kn_ref, vn_ref, lfn_ref, *rest, n_pages, n_seq):
    n_blk = n_seq * n_pages
    kp = rest[0:n_blk]
    vp = rest[n_blk:2 * n_blk]
    lp = rest[2 * n_blk:3 * n_blk]
    o_ref = rest[3 * n_blk]
    pw = PAGE_SIZE * HA
    total = n_pages * pw
    past = n_pages * PAGE_SIZE

    rowh = lax.broadcasted_iota(jnp.int32, (SUBLANES, total), 0)
    lane = lax.broadcasted_iota(jnp.int32, (SUBLANES, total), 1)
    own = (lane % HA) == rowh
    pos = lax.broadcasted_iota(jnp.int32, (SUBLANES, past), 1)
    e_row = lax.broadcasted_iota(jnp.int32, (PAGE_SIZE, pw), 0)
    e_col = lax.broadcasted_iota(jnp.int32, (PAGE_SIZE, pw), 1)
    spread = jnp.where(e_col // HA == e_row, 1.0, 0.0).astype(BF16)

    for sq in range(n_seq):
        pages = range(sq * n_pages, (sq + 1) * n_pages)
        q = q_ref[sq]
        qb = q.astype(BF16)

        lf = jnp.concatenate([lp[j][0, 0] for j in pages], axis=1)
        lf = jnp.concatenate([lf, jnp.zeros_like(lf)], axis=0)
        suf = lf
        s = 1
        while s < past:
            suf = suf + jnp.where(pos + s < past, pltpu.roll(suf, past - s, 1), 0.0)
            s *= 2
        bias_hm = (suf - lf + lfn_ref[sq]) * LOG2E

        stacked = jnp.concatenate(
            [bias_hm[:, j * PAGE_SIZE:(j + 1) * PAGE_SIZE] for j in range(n_pages)], axis=0)
        hi = stacked.astype(BF16)
        r1 = stacked - hi.astype(F32)
        mid = r1.astype(BF16)
        lo = (r1 - mid.astype(F32)).astype(BF16)
        bias = _dot(hi, spread) + _dot(mid, spread) + _dot(lo, spread)

        logits = jnp.concatenate(
            [_dot_nt(qb, kp[j][0, 0].astype(BF16)) + bias[t * SUBLANES:(t + 1) * SUBLANES, :]
             for t, j in enumerate(pages)], axis=1)
        logits = jnp.where(own, logits, NEG)
        s_new = jnp.sum(q * kn_ref[sq], axis=-1, keepdims=True)
        m = jnp.maximum(jnp.max(logits, axis=-1, keepdims=True), s_new)
        p = jnp.exp2(logits - m)
        p_new = jnp.exp2(s_new - m)
        l = jnp.sum(p, axis=-1, keepdims=True) + p_new
        pb = p.astype(BF16)
        acc = p_new * vn_ref[sq]
        for t, j in enumerate(pages):
            acc = acc + _dot(pb[:, t * pw:(t + 1) * pw], vp[j][0, 0].astype(BF16))
        o_ref[sq] = acc / l


def _fox_sample(page_table, q, k_new, v_new, lf_new, pool_k, pool_v, pool_lf, layer):
    n, n_pages = page_table.shape
    pt = page_table.reshape(-1)
    pw = PAGE_SIZE * HA
    n_seq = _SEQ_PER_STEP if n % _SEQ_PER_STEP == 0 else 1
    row = pl.BlockSpec((n_seq, SUBLANES, DA), lambda i, pt: (i, 0, 0))

    def pages(width_shape):
        return [pl.BlockSpec((1, 1) + width_shape,
                             lambda i, pt, sq=sq, j=j: (layer, pt[(i * n_seq + sq) * n_pages + j], 0, 0))
                for sq in range(n_seq) for j in range(n_pages)]

    in_specs = ([row, row, row, pl.BlockSpec((n_seq, SUBLANES, 1), lambda i, pt: (i, 0, 0))]
                + pages((pw, DA)) + pages((pw, DA)) + pages((HA, PAGE_SIZE)))
    grid_spec = pltpu.PrefetchScalarGridSpec(
        num_scalar_prefetch=1, grid=(n // n_seq,), in_specs=in_specs,
        out_specs=pl.BlockSpec((n_seq, SUBLANES, DA), lambda i, pt: (i, 0, 0)))
    n_blk = n_seq * n_pages
    return pl.pallas_call(
        functools.partial(_paged_kernel, n_pages=n_pages, n_seq=n_seq),
        grid_spec=grid_spec,
        out_shape=jax.ShapeDtypeStruct((n, SUBLANES, DA), F32),
        compiler_params=_params(("arbitrary",)),
        name="fox_sample",
    )(pt, q, k_new, v_new, lf_new, *([pool_k] * n_blk), *([pool_v] * n_blk), *([pool_lf] * n_blk))


def _sret_kernel(q_ref, k_ref, v_ref, gate_ref, gn_ref, s0_ref, o_ref, sn_ref):
    q = q_ref[0]
    k = k_ref[0]
    v = v_ref[0]
    hw = DKB * DVB
    lane = lax.broadcasted_iota(jnp.int32, (DKB, hw), 1)
    rowi = lax.broadcasted_iota(jnp.int32, (DKB, hw), 0)
    rep = (lane // DVB == rowi).astype(BF16)
    til = (lane % DVB == rowi).astype(BF16)
    outs = []
    for h in range(HB):
        sl = slice(h * DKB, (h + 1) * DKB)
        g = math.exp(_LOG_G[h])
        sn = g * s0_ref[0, :, h * hw:(h + 1) * hw] + _dot(k[:, sl], rep) * _dot(v[:, sl], til)
        sn_ref[0, :, h * hw:(h + 1) * hw] = sn
        prod = _dot(q[:, sl], rep) * sn
        acc = prod[:, 0:LANES]
        for j in range(1, hw // LANES):
            acc = acc + prod[:, j * LANES:(j + 1) * LANES]
        oh = acc[:, 0:DVB] + acc[:, DVB:LANES]
        mu = jnp.mean(oh, axis=-1, keepdims=True)
        xc = oh - mu
        outs.append(xc * lax.rsqrt(jnp.mean(xc * xc, axis=-1, keepdims=True) + EPS))
    r = jnp.concatenate(outs, axis=1) * gn_ref[...]
    o_ref[0] = (r * gate_ref[0].astype(F32)).astype(BF16)


def _ret_sample(rq, rk, rv, gate, gn, state, layer, nb):
    _, n, _ = rq.shape
    sw = HB * DKB * DVB
    rows = pl.BlockSpec((1, nb, WB), lambda i: (0, i, 0))
    return pl.pallas_call(
        _sret_kernel,
        grid=(n // nb,),
        in_specs=[rows, rows, rows, rows, _const_spec((1, WB)),
                  pl.BlockSpec((1, nb, sw), lambda i: (layer, i, 0))],
        out_specs=[rows, pl.BlockSpec((1, nb, sw), lambda i: (0, i, 0))],
        out_shape=[jax.ShapeDtypeStruct((1, n, WB), BF16), jax.ShapeDtypeStruct((1, n, sw), F32)],
        compiler_params=_params(("arbitrary",)),
        name="ret_sample",
    )(rq, rk, rv, gate, gn, state)


def _sconv_kernel(ctx_ref, u_ref, w_ref, b_ref, lg_ref, lb_ref, o_ref, ns_ref):
    ctx = ctx_ref[0]
    u = u_ref[...]
    w = w_ref[...]
    cv = (jnp.sum(ctx * w[0:CONV_W - 1, :][None], axis=1, keepdims=True)
          + u * w[CONV_W - 1:CONV_W, :][None] + b_ref[...][None])
    o_ref[...] = _silu(_layernorm(cv, lg_ref[...][None], lb_ref[...][None])).astype(BF16)
    ns_ref[0, :, 0:CONV_W - 2, :] = ctx[:, 1:CONV_W - 1, :]
    ns_ref[0, :, CONV_W - 2:CONV_W - 1, :] = u


def _conv_sample(state, u, w, b, lg, lb, layer, nb):
    n = u.shape[0]
    return pl.pallas_call(
        _sconv_kernel,
        grid=(n // nb,),
        in_specs=[pl.BlockSpec((1, nb, CONV_W - 1, DC), lambda i: (layer, i, 0, 0)),
                  pl.BlockSpec((nb, 1, DC), lambda i: (i, 0, 0)),
                  _const_spec((CONV_W, DC)), _const_spec((1, DC)), _const_spec((1, DC)), _const_spec((1, DC))],
        out_specs=[pl.BlockSpec((nb, 1, DC), lambda i: (i, 0, 0)),
                   pl.BlockSpec((1, nb, CONV_W - 1, DC), lambda i: (0, i, 0, 0))],
        out_shape=[jax.ShapeDtypeStruct((n, 1, DC), BF16),
                   jax.ShapeDtypeStruct((1, n, CONV_W - 1, DC), F32)],
        compiler_params=_params(("arbitrary",)),
        name="conv_sample",
    )(state, u, w, b, lg, lb)


def _reorder_w_in(w):
    wt = w.T
    o = 3 * WA
    tail = jnp.pad(wt[o:o + HA], ((0, LANES - HA), (0, 0)))
    return jnp.concatenate([wt[:o], wt[o + HA:], tail], axis=0).astype(BF16)


def _pick(total, pref):
    t = min(total, pref)
    assert total % t == 0
    return t


def kernel(x_prompt, x_sample, cache_k, cache_v, cache_logf, state_ret, state_conv, state_ffn_conv, page_table,
           c_prompt, c_sample, ada_w, ada_b, norms, w_in, b_f, ret_gn_g, conv_w, conv_b, conv_ln_g, conv_ln_b,
           w_o, ffn_up, ffn_conv_w, ffn_conv_b, ffn_down):
    depth = ada_w.shape[0]
    bp, sp, d = x_prompt.shape
    ns, ts, _ = x_sample.shape
    assert ts == 1 and d == D_MODEL
    n_pool = cache_k.shape[1]
    n_pages = page_table.shape[1]

    n_c = ns + bp
    n_c_pad = -(-n_c // SUBLANES) * SUBLANES
    c_all = jnp.pad(jnp.concatenate([c_sample, c_prompt], axis=0), ((0, n_c_pad - n_c), (0, 0)))
    mod_all = _ada_mod(c_all, ada_w, ada_b)

    cos_p, sin_p = _rope_tables(sp, 0, 1)
    cos_s, sin_s = _rope_tables(ns, n_pages * PAGE_SIZE, 0)

    pool_k = cache_k.reshape(depth, n_pool, PAGE_SIZE * HA, DA)
    pool_v = cache_v.reshape(depth, n_pool, PAGE_SIZE * HA, DA)
    pool_lf = cache_logf.transpose(0, 1, 3, 2)
    st_ret = state_ret.reshape(depth, ns, HB * DKB * DVB)
    st_ffn = state_ffn_conv.reshape(depth, ns, 4 * D_FF)

    tm_p = _pick(sp, 512)
    tq = _pick(sp, 1024)
    chunk = _pick(sp, 128)
    nb_s = _pick(ns, 32)

    xp = x_prompt
    xs = x_sample.reshape(1, ns, d)
    p_out, s_out = [], []
    kv_p = kv_s = None
    wo, wup, wdn = w_o.astype(BF16), ffn_up.astype(BF16), ffn_down.astype(BF16)

    def heads_on_rows(a):
        return jnp.pad(a.reshape(ns, HA, DA), ((0, 0), (0, SUBLANES - HA), (0, 0)))

    for l in range(depth):
        w_re = _reorder_w_in(w_in[l])
        bf_pad = jnp.pad(b_f[l], (0, LANES - HA)).reshape(1, LANES)
        n0 = norms[l, 0:1]
        gn = ret_gn_g[l].reshape(1, WB)
        cb = conv_b[l].reshape(1, DC)
        clg = conv_ln_g[l].reshape(1, DC)
        clb = conv_ln_b[l].reshape(1, DC)
        fcb = ffn_conv_b[l].reshape(1, 2 * D_FF)
        mod_s = mod_all[l, 0:ns].reshape(1, ns, 6 * d)
        mod_p = mod_all[l, ns:ns + bp].reshape(bp, 1, 6 * d)

        ko, vo, qa, ka, va, lft, rq, rk, rv, gate, u = _in_proj(
            xp, mod_p, n0, w_re, bf_pad, cos_p, sin_p, tm_p, depth, l, kv_p)
        kv_p = (ko, vo)
        oa = _fox_prompt(qa, ka, va, _cumsum(lft), tq)
        ob, s_bd = _ret_prompt(rq, rk, rv, gate, gn, chunk)
        oc, conv_st = _conv_prompt(u, conv_w[l], cb, clg, clb, tm_p)
        xp, ffn_st = _out_ffn(xp, oa, ob, oc, mod_p, norms[l], wo, wup, ffn_conv_w[l], fcb, wdn, tm_p, layer=l)
        s5 = s_bd.reshape(bp, HB, DKB, HB, DVB)
        ret_st = jnp.stack([s5[:, h, :, h, :] for h in range(HB)], axis=1)
        p_out.append((lft[:, 0:HA, :].transpose(0, 2, 1), ret_st, conv_st, ffn_st))

        ko, vo, qa, ka, va, lft, rq, rk, rv, gate, u = _in_proj(
            xs, mod_s, n0, w_re, bf_pad, cos_s, sin_s, ns, depth, l, kv_s)
        kv_s = (ko, vo)
        lfn = lft[0].T.reshape(ns, SUBLANES, 1)
        oa = _fox_sample(page_table, heads_on_rows(qa.astype(F32)), heads_on_rows(ko[l, 0]),
                         heads_on_rows(vo[l, 0]), lfn, pool_k, pool_v, pool_lf, l)
        oa = oa[:, 0:HA, :].reshape(1, ns, WA).astype(BF16)
        ob, ret_new = _ret_sample(rq, rk, rv, gate, gn, st_ret, l, nb_s)
        oc, conv_new = _conv_sample(state_conv, u.reshape(ns, 1, DC), conv_w[l], cb, clg, clb, l, nb_s)
        xs, ffn_new = _out_ffn(xs, oa, ob, oc.reshape(1, ns, DC), mod_s, norms[l],
                               wo, wup, ffn_conv_w[l], fcb, wdn, ns, state=st_ffn, layer=l)
        s_out.append((lft[0, 0:HA, :].T.reshape(ns, 1, HA), ret_new.reshape(ns, HB, DKB, DVB),
                      conv_new[0], ffn_new.reshape(ns, 2, 2 * D_FF)))

    plf, pret, pconv, pffn = [jnp.stack(a) for a in zip(*p_out)]
    slf, sret, sconv, sffn = [jnp.stack(a) for a in zip(*s_out)]
    pk, pv = [a.reshape(depth, bp, sp, HA, DA) for a in kv_p]
    sk, sv = [a.reshape(depth, ns, 1, HA, DA) for a in kv_s]
    return (xp, xs.reshape(ns, 1, d), pk, pv, plf, pret, pconv, pffn, sk, sv, slf, sret, sconv, sffn)
```

```python
import functools
import math

import jax
import jax.numpy as jnp
import numpy as np
from jax import lax
from jax.experimental import pallas as pl
from jax.experimental.pallas import tpu as pltpu

F32 = jnp.float32
BF16 = jnp.bfloat16

D_MODEL = 1024
HA, DA = 4, 128
WA = HA * DA
HB, DKB, DVB = 4, 64, 64
WB = HB * DVB
DC = 256
CONV_W = 31
D_FF = 2816
PAGE_SIZE = 128
ROPE_BASE = 10000.0
EPS = 1e-6
NEG = -1e30
LOG2E = math.log2(math.e)

LANES = 128
SUBLANES = 8
VMEM_LIMIT = 56 * 1024 * 1024

_OQ, _OK, _OV = 0, 512, 1024
_OBQ, _OBK, _OBV, _OBG = 1536, 1792, 2048, 2304
_OCA, _OCB, _OAF = 2560, 2816, 3072
N_IN_PAD = 3200

_LOG_G = [math.log1p(-(2.0 ** (-5.0 - h))) for h in range(HB)]


def _sigmoid(x):
    return 1.0 / (1.0 + jnp.exp(-x))


def _silu(x):
    return x * _sigmoid(x)


def _log_sigmoid(x):
    return jnp.minimum(x, 0.0) - jnp.log1p(jnp.exp(-jnp.abs(x)))


def _rms(x):
    return x * lax.rsqrt(jnp.mean(x * x, axis=-1, keepdims=True) + EPS)


def _dot(a, b):
    return jnp.dot(a, b, preferred_element_type=F32)


def _dot_nt(a, b):
    return lax.dot_general(a, b, (((1,), (1,)), ((), ())), preferred_element_type=F32)


def _params(sem):
    return pltpu.CompilerParams(dimension_semantics=sem, vmem_limit_bytes=VMEM_LIMIT)


def _const_spec(shape):
    nd = len(shape)
    return pl.BlockSpec(shape, lambda *_: (0,) * nd, pipeline_mode=pl.Buffered(1))


def _layer_spec(shape, layer):
    nd = len(shape)
    return pl.BlockSpec((1,) + tuple(shape), lambda *_: (layer,) + (0,) * nd, pipeline_mode=pl.Buffered(1))


def _ada_kernel(c_ref, w_ref, b_ref, o_ref):
    s = _silu(c_ref[...]).astype(BF16)
    o_ref[0] = _dot(s, w_ref[0].astype(BF16)) + b_ref[0]


def _ada_mod(c_all, ada_w, ada_b):
    depth, d, n6 = ada_w.shape
    r = c_all.shape[0]
    tn = 1536
    return pl.pallas_call(
        _ada_kernel,
        grid=(depth, n6 // tn),
        in_specs=[pl.BlockSpec((r, d), lambda l, j: (0, 0)),
                  pl.BlockSpec((1, d, tn), lambda l, j: (l, 0, j)),
                  pl.BlockSpec((1, 1, tn), lambda l, j: (l, 0, j))],
        out_specs=pl.BlockSpec((1, r, tn), lambda l, j: (l, 0, j)),
        out_shape=jax.ShapeDtypeStruct((depth, r, n6), F32),
        compiler_params=_params(("arbitrary", "arbitrary")),
        name="ada_mod",
    )(c_all, ada_w, ada_b.reshape(depth, 1, n6))


def _rope_kernel(inv_ref, cos_ref, sin_ref, *, pos0, stride, tm):
    i = pl.program_id(0)
    row = lax.broadcasted_iota(jnp.int32, (tm, LANES), 0) + i * tm
    lane = lax.broadcasted_iota(jnp.int32, (tm, LANES), 1)
    ang = (pos0 + stride * row).astype(F32) * inv_ref[...]
    cos_ref[...] = jnp.cos(ang)
    s = jnp.sin(ang)
    sin_ref[...] = jnp.where((lane % DKB) < DKB // 2, -s, s)


def _rope_tables(length, pos0, stride):
    half = DKB // 2
    inv = ROPE_BASE ** (-jnp.arange(half, dtype=F32) / half)
    inv = jnp.tile(inv, LANES // half).reshape(1, LANES)
    tm = min(length, 1024)
    out = jax.ShapeDtypeStruct((length, LANES), F32)
    return pl.pallas_call(
        functools.partial(_rope_kernel, pos0=pos0, stride=stride, tm=tm),
        grid=(length // tm,),
        in_specs=[pl.BlockSpec((1, LANES), lambda i: (0, 0))],
        out_specs=[pl.BlockSpec((tm, LANES), lambda i: (i, 0))] * 2,
        out_shape=[out, out],
        compiler_params=_params(("arbitrary",)),
        name="rope_tables",
    )(inv)


def _in_kernel(*refs, tm, aliased, conv):
    n_in = 12 if conv else 8
    ins, rest = refs[:n_in], refs[n_in + (2 if aliased else 0):]
    x_ref, sh_ref, sc_ref, g_ref, w_ref, bf_ref, cos_ref, sin_ref = ins[:8]
    ko_ref, vo_ref, qa_ref, ka_ref, va_ref, lft_ref, rq_ref, rk_ref, rv_ref, gate_ref, u_ref = rest[:11]
    x = x_ref[0]
    h = _rms(x) * g_ref[...]
    hb = (h * (1.0 + sc_ref[0]) + sh_ref[0]).astype(BF16)

    def proj(off, width):
        return _dot_nt(hb, w_ref[off:off + width, :])

    qa_ref[0] = (proj(_OQ, WA) * (DA ** -0.5 * LOG2E)).astype(BF16)
    k = proj(_OK, WA)
    ka_ref[0] = k.astype(BF16)
    v = proj(_OV, WA)
    va_ref[0] = v.astype(BF16)
    for hd in range(HA):
        ko_ref[0, 0, pl.ds(hd, tm, stride=HA), :] = k[:, hd * DA:(hd + 1) * DA]
        vo_ref[0, 0, pl.ds(hd, tm, stride=HA), :] = v[:, hd * DA:(hd + 1) * DA]

    lf = _log_sigmoid(proj(_OAF, LANES) + bf_ref[...])
    lft_ref[0] = lf.T[0:SUBLANES, :]

    cos = cos_ref[...]
    sin = sin_ref[...]
    lane = lax.broadcasted_iota(jnp.int32, cos.shape, 1)
    first = (lane % DKB) < DKB // 2

    def rope(t):
        partner = jnp.where(first, pltpu.roll(t, LANES - DKB // 2, 1), pltpu.roll(t, DKB // 2, 1))
        return t * cos + partner * sin

    bq = proj(_OBQ, WB)
    bk = proj(_OBK, WB)
    for j in range(WB // LANES):
        sl = slice(j * LANES, (j + 1) * LANES)
        rq_ref[0, :, sl] = rope(bq[:, sl]).astype(BF16)
        rk_ref[0, :, sl] = (rope(bk[:, sl]) * (DKB ** -0.5)).astype(BF16)
    rv_ref[0] = proj(_OBV, WB).astype(BF16)
    gate_ref[0] = _silu(proj(_OBG, WB)).astype(BF16)
    u = proj(_OCA, DC) * _sigmoid(proj(_OCB, DC))
    if conv:
        _conv_tile(u, *ins[8:12], u_ref, rest[11], rest[12], tm)
    else:
        u_ref[0] = u


def _mod_spec(mod, tm, col):
    rows = mod.shape[1]
    if rows == 1:
        return pl.BlockSpec((1, 1, D_MODEL), lambda b, i: (b, 0, col))
    return pl.BlockSpec((1, tm, D_MODEL), lambda b, i: (b, i, col))


def _in_proj(x, mod, norm_g, w_re, bf_pad, cos, sin, tm, depth, layer, kv_prev, conv=None):
    bsz, length, d = x.shape
    grid = (bsz, length // tm)

    def rows(width):
        return pl.BlockSpec((1, tm, width), lambda b, i: (b, i, 0))

    def shp(width, dt):
        return jax.ShapeDtypeStruct((bsz, length, width), dt)

    tab = pl.BlockSpec((tm, LANES), lambda b, i: (i, 0))
    cache_spec = pl.BlockSpec((1, 1, tm * HA, DA), lambda b, i: (layer, b, i, 0))
    cache_shape = jax.ShapeDtypeStruct((depth, bsz, length * HA, DA), F32)
    in_specs = [rows(d), _mod_spec(mod, tm, 0), _mod_spec(mod, tm, 1),
                _const_spec((1, d)), _const_spec((N_IN_PAD, d)), _const_spec((1, LANES)), tab, tab]
    args = [x, mod, mod, norm_g, w_re, bf_pad, cos, sin]
    out_specs = [cache_spec, cache_spec, rows(WA), rows(WA), rows(WA),
                 pl.BlockSpec((1, SUBLANES, tm), lambda b, i: (b, 0, i)),
                 rows(WB), rows(WB), rows(WB), rows(WB), rows(DC)]
    out_shape = [cache_shape, cache_shape, shp(WA, BF16), shp(WA, BF16), shp(WA, BF16),
                 jax.ShapeDtypeStruct((bsz, SUBLANES, length), F32),
                 shp(WB, BF16), shp(WB, BF16), shp(WB, BF16), shp(WB, BF16),
                 shp(DC, F32 if conv is None else BF16)]
    scratch = []
    if conv is not None:
        in_specs += [_const_spec((CONV_W, DC))] + [_const_spec((1, DC))] * 3
        args += list(conv)
        out_specs.append(pl.BlockSpec((1, CONV_W - 1, DC), lambda b, i: (b, 0, 0)))
        out_shape.append(jax.ShapeDtypeStruct((bsz, CONV_W - 1, DC), F32))
        scratch = [pltpu.VMEM((SUBLANES, tm + _CONV_PAD, DC), F32)]
    aliases = {}
    if kv_prev is not None:
        aliases = {len(args): 0, len(args) + 1: 1}
        in_specs += [pl.BlockSpec(memory_space=pl.ANY)] * 2
        args += list(kv_prev)
    return pl.pallas_call(
        functools.partial(_in_kernel, tm=tm, aliased=kv_prev is not None, conv=conv is not None),
        grid=grid,
        in_specs=in_specs,
        out_specs=out_specs,
        out_shape=out_shape,
        scratch_shapes=scratch,
        input_output_aliases=aliases,
        compiler_params=_params(("arbitrary", "arbitrary")),
        name="in_proj",
    )(*args)


def _cumsum_kernel(x_ref, o_ref):
    x = x_ref[0]
    length = x.shape[1]
    lane = lax.broadcasted_iota(jnp.int32, x.shape, 1)
    s = 1
    while s < length:
        x = x + jnp.where(lane >= s, pltpu.roll(x, s, 1), 0.0)
        s *= 2
    o_ref[0] = x * LOG2E


def _cumsum(lft):
    bsz, r, length = lft.shape
    return pl.pallas_call(
        _cumsum_kernel,
        grid=(bsz,),
        in_specs=[pl.BlockSpec((1, r, length), lambda b: (b, 0, 0))],
        out_specs=pl.BlockSpec((1, r, length), lambda b: (b, 0, 0)),
        out_shape=jax.ShapeDtypeStruct(lft.shape, F32),
        compiler_params=_params(("arbitrary",)),
        name="logf_cumsum",
    )(lft)


def _fox_kernel(q_ref, k_ref, v_ref, ck_ref, o_ref, *, tq):
    qi = pl.program_id(1)
    half = tq // 2
    ones = jnp.ones((tq, DA), BF16)

    def update(m, acc, h, q_rows, off, n_keys, ck, mask):
        sl = slice(h * DA, (h + 1) * DA)
        k = k_ref[0, pl.ds(off, n_keys), sl]
        v1 = jnp.concatenate([v_ref[0, pl.ds(off, n_keys), sl], ones[0:n_keys]], axis=1)
        s = _dot_nt(q_ref[0, q_rows, sl], k) - ck
        if mask is not None:
            s = jnp.where(mask, s, NEG)
        m_new = jnp.maximum(m, jnp.max(s, axis=-1, keepdims=True))
        p = jnp.exp2(s - m_new)
        return m_new, jnp.exp2(m - m_new) * acc + _dot(p.astype(BF16), v1)

    def step(kb, carry):
        off = pl.multiple_of(kb * tq, tq)
        return tuple(update(*carry[h], h, slice(0, tq), off, tq, ck_ref[0, kb, h:h + 1, :], None)
                     for h in range(HA))

    def diagonal(carry):
        off = pl.multiple_of(qi * tq, tq)
        row = lax.broadcasted_iota(jnp.int32, (half, tq), 0)
        col = lax.broadcasted_iota(jnp.int32, (half, tq), 1)
        row_a = lax.broadcasted_iota(jnp.int32, (half, half), 0)
        col_a = lax.broadcasted_iota(jnp.int32, (half, half), 1)
        out = []
        for h in range(HA):
            m, acc = carry[h]
            m_a, acc_a = update(m[0:half], acc[0:half], h, slice(0, half), off, half,
                                ck_ref[0, qi, h:h + 1, 0:half], row_a >= col_a)
            m_b, acc_b = update(m[half:tq], acc[half:tq], h, slice(half, tq), off, tq,
                                ck_ref[0, qi, h:h + 1, :], row + half >= col)
            out.append((jnp.concatenate([m_a, m_b], axis=0), jnp.concatenate([acc_a, acc_b], axis=0)))
        return tuple(out)

    init = tuple((jnp.full((tq, 1), NEG, F32), jnp.zeros((tq, 2 * DA), F32)) for _ in range(HA))
    carry = diagonal(lax.fori_loop(0, qi, step, init))
    for h in range(HA):
        acc = carry[h][1]
        o_ref[0, :, h * DA:(h + 1) * DA] = (acc[:, 0:DA] / acc[:, DA:2 * DA]).astype(BF16)


def _fox_prompt(qa, ka, va, cum, tq):
    bsz, length, _ = qa.shape
    nb = length // tq
    ck = cum.reshape(bsz, SUBLANES, nb, tq).transpose(0, 2, 1, 3)
    full = pl.BlockSpec((1, length, WA), lambda b, i: (b, 0, 0), pipeline_mode=pl.Buffered(1))
    return pl.pallas_call(
        functools.partial(_fox_kernel, tq=tq),
        grid=(bsz, nb),
        in_specs=[pl.BlockSpec((1, tq, WA), lambda b, i: (b, i, 0)), full, full,
                  pl.BlockSpec((1, nb, SUBLANES, tq), lambda b, i: (b, 0, 0, 0))],
        out_specs=pl.BlockSpec((1, tq, WA), lambda b, i: (b, i, 0)),
        out_shape=jax.ShapeDtypeStruct((bsz, length, WA), BF16),
        compiler_params=_params(("arbitrary", "arbitrary")),
        name="fox_prompt",
    )(qa, ka, va, ck)


def _head_mean(z, avg):
    hi = z.astype(BF16)
    lo = (z - hi.astype(F32)).astype(BF16)
    return _dot(hi, avg) + _dot(lo, avg)


def _group_avg_matrix():
    r = lax.broadcasted_iota(jnp.int32, (WB, WB), 0) // DVB
    c = lax.broadcasted_iota(jnp.int32, (WB, WB), 1) // DVB
    return r == c


def _lane_log_g(shape, axis):
    head = lax.broadcasted_iota(jnp.int32, shape, axis) // DVB
    lg = jnp.full(shape, _LOG_G[HB - 1], F32)
    for h in range(HB - 1):
        lg = jnp.where(head == h, _LOG_G[h], lg)
    return lg


def _ret_kernel(q_ref, k_ref, v_ref, gate_ref, gn_ref, o_ref, so_ref,
                s_ref, dec_ref, xi_ref, zeta_ref, gc_ref, *, chunk, bsz):
    ci = pl.program_id(0)
    same_head = _group_avg_matrix()

    @pl.when(ci == 0)
    def _():
        s_ref[...] = jnp.zeros_like(s_ref)
        r_i = lax.broadcasted_iota(jnp.int32, (chunk, chunk), 0)
        c_i = lax.broadcasted_iota(jnp.int32, (chunk, chunk), 1)
        rel = (r_i - c_i).astype(F32)
        for h in range(HB):
            dec_ref[h] = jnp.where(rel >= 0, jnp.exp(_LOG_G[h] * jnp.maximum(rel, 0.0)), 0.0)
        lg = _lane_log_g((chunk, WB), 1)
        idx = lax.broadcasted_iota(jnp.int32, (chunk, WB), 0).astype(F32)
        xi_ref[...] = jnp.exp(lg * (idx + 1.0))
        zeta_ref[...] = jnp.exp(lg * (chunk - 1.0 - idx))
        gc_ref[...] = jnp.exp(_lane_log_g((WB, WB), 0) * float(chunk))

    head = lax.broadcasted_iota(jnp.int32, (1, WB), 1) // DVB
    avg = jnp.where(same_head, 1.0 / DVB, 0.0).astype(BF16)
    for b in range(bsz):
        q = q_ref[b]
        k = k_ref[b]
        v = v_ref[b]
        o = jnp.zeros((chunk, WB), F32)
        for h in range(HB):
            mh = head == h
            att = _dot_nt(jnp.where(mh, q, jnp.zeros_like(q)), k) * dec_ref[h]
            o = o + jnp.where(mh, _dot(att.astype(BF16), v), 0.0)
        s_old = s_ref[b]
        o = o + _dot(q, s_old.astype(BF16)) * xi_ref[...]
        kz_t = (k.astype(F32) * zeta_ref[...]).T.astype(BF16)
        s_ref[b] = jnp.where(same_head, gc_ref[...] * s_old + _dot(kz_t, v), 0.0)

        xc = o - _head_mean(o, avg)
        var = _head_mean(xc * xc, avg)
        r = xc * lax.rsqrt(var + EPS) * gn_ref[...]
        o_ref[b] = (r * gate_ref[b].astype(F32)).astype(BF16)

    @pl.when(ci == pl.num_programs(0) - 1)
    def _():
        so_ref[...] = s_ref[...]


def _ret_prompt(rq, rk, rv, gate, gn, chunk):
    bsz, length, _ = rq.shape
    rows = pl.BlockSpec((bsz, chunk, WB), lambda i: (0, i, 0))
    return pl.pallas_call(
        functools.partial(_ret_kernel, chunk=chunk, bsz=bsz),
        grid=(length // chunk,),
        in_specs=[rows, rows, rows, rows, _const_spec((1, WB))],
        out_specs=[rows, pl.BlockSpec((bsz, WB, WB), lambda i: (0, 0, 0))],
        out_shape=[jax.ShapeDtypeStruct((bsz, length, WB), BF16),
                   jax.ShapeDtypeStruct((bsz, WB, WB), F32)],
        scratch_shapes=[pltpu.VMEM((bsz, WB, WB), F32), pltpu.VMEM((HB, chunk, chunk), F32),
                        pltpu.VMEM((chunk, WB), F32), pltpu.VMEM((chunk, WB), F32), pltpu.VMEM((WB, WB), F32)],
        compiler_params=_params(("arbitrary",)),
        name="ret_prompt",
    )(rq, rk, rv, gate, gn)


_CONV_PAD = 32


def _layernorm(x, g, b):
    mu = jnp.mean(x, axis=-1, keepdims=True)
    xc = x - mu
    return xc * lax.rsqrt(jnp.mean(xc * xc, axis=-1, keepdims=True) + EPS) * g + b


_CONV_ROWS = 128


def _conv_tile(u, w_ref, b_ref, lg_ref, lb_ref, o_ref, st_ref, ext_ref, tm):
    i = pl.program_id(1)

    @pl.when(i == 0)
    def _():
        ext_ref[0, 0:_CONV_PAD, :] = jnp.zeros((_CONV_PAD, DC), F32)

    ext_ref[0, _CONV_PAD:_CONV_PAD + tm, :] = u
    span = tm + _CONV_PAD - SUBLANES
    for r in range(1, SUBLANES):
        ext_ref[r, 0:span, :] = ext_ref[0, r:r + span, :]

    base = _CONV_PAD - (CONV_W - 1)
    for rb in range(tm // _CONV_ROWS):
        acc = jnp.zeros((_CONV_ROWS, DC), F32) + b_ref[...]
        for j in range(CONV_W):
            a, r = divmod(base + j, SUBLANES)
            start = a * SUBLANES + rb * _CONV_ROWS
            acc = acc + w_ref[j:j + 1, :] * ext_ref[r, start:start + _CONV_ROWS, :]
        rows = slice(rb * _CONV_ROWS, (rb + 1) * _CONV_ROWS)
        o_ref[0, rows, :] = _silu(_layernorm(acc, lg_ref[...], lb_ref[...])).astype(BF16)

    @pl.when(i == pl.num_programs(1) - 1)
    def _():
        st_ref[0] = ext_ref[0, tm + base:tm + _CONV_PAD, :]

    ext_ref[0, 0:_CONV_PAD, :] = ext_ref[0, tm:tm + _CONV_PAD, :]


_FF_CHUNK = 256
_FF_BUFS = 4
_FF_GROUP = D_FF // _FF_CHUNK


def _ffn_kernel(*refs, tm, step_mode):
    if step_mode:
        (x_ref, oa_ref, ob_ref, oc_ref, g1_ref, sh2_ref, sc2_ref, g2_ref, n_ref,
         wo_ref, wup_ref, cw_ref, cb_ref, wdn_ref, st_ref, y_ref, ns_ref, act_ref) = refs
    else:
        (x_ref, oa_ref, ob_ref, oc_ref, g1_ref, sh2_ref, sc2_ref, g2_ref, n_ref,
         wo_ref, wup_ref, cw_ref, cb_ref, wdn_ref, y_ref, ns_ref, act_ref, ctx_ref, bufs_ref) = refs
        i = pl.program_id(1)

        @pl.when(i == 0)
        def _():
            ctx_ref[...] = jnp.zeros_like(ctx_ref)

    wo_ref, wup_ref, wdn_ref = wo_ref.at[0], wup_ref.at[0], wdn_ref.at[0]
    mix = (_dot(oa_ref[0], wo_ref[0:WA, :]) + _dot(ob_ref[0], wo_ref[WA:WA + WB, :])
           + _dot(oc_ref[0], wo_ref[WA + WB:, :]))
    x1 = x_ref[0] + g1_ref[0] * (_rms(mix) * n_ref[1:2, :])
    h2 = ((_rms(x1) * n_ref[2:3, :]) * (1.0 + sc2_ref[0]) + sh2_ref[0]).astype(BF16)

    n_chunks = D_FF // _FF_CHUNK
    f = None
    for c in range(n_chunks):
        halves = []
        for half in range(2):
            c0 = half * D_FF + c * _FF_CHUNK
            cs = slice(c0, c0 + _FF_CHUNK)
            up = _dot(h2, wup_ref[:, cs])
            if step_mode:
                prev2 = st_ref[0, :, cs]
                prev1 = st_ref[0, :, 2 * D_FF + c0:2 * D_FF + c0 + _FF_CHUNK]
                ns_ref[0, :, cs] = prev1
                ns_ref[0, :, 2 * D_FF + c0:2 * D_FF + c0 + _FF_CHUNK] = up
            else:
                buf_ref = bufs_ref.at[(2 * c + half) % _FF_BUFS]
                buf_ref[0:SUBLANES, :] = ctx_ref[:, cs]
                buf_ref[SUBLANES:SUBLANES + tm, :] = up
                prev1 = buf_ref[SUBLANES - 1:SUBLANES - 1 + tm, :]
                prev2 = buf_ref[SUBLANES - 2:SUBLANES - 2 + tm, :]
                ctx_ref[:, cs] = buf_ref[tm:tm + SUBLANES, :]
            halves.append(cw_ref[0:1, cs] * prev2 + cw_ref[1:2, cs] * prev1
                          + cw_ref[2:3, cs] * up + cb_ref[:, cs])
        act_ref[:, c * _FF_CHUNK:(c + 1) * _FF_CHUNK] = (_silu(halves[0]) * halves[1]).astype(BF16)
        if (c + 1) % _FF_GROUP == 0 or c + 1 == n_chunks:
            lo = (c // _FF_GROUP) * _FF_GROUP * _FF_CHUNK
            part = _dot(act_ref[:, lo:(c + 1) * _FF_CHUNK], wdn_ref[lo:(c + 1) * _FF_CHUNK, :])
            f = part if f is None else f + part
    y_ref[0] = x1 + g2_ref[0] * (_rms(f) * n_ref[3:4, :])

    if not step_mode:
        @pl.when(i == pl.num_programs(1) - 1)
        def _():
            ns_ref[0] = ctx_ref[SUBLANES - 2:SUBLANES, :]


def _out_ffn(x, oa, ob, oc, mod, norms, wo, wup, cw, cb, wdn, tm, state=None, layer=0):
    bsz, length, d = x.shape
    step_mode = state is not None

    def rows(width):
        return pl.BlockSpec((1, tm, width), lambda b, i: (b, i, 0))

    in_specs = [rows(d), rows(WA), rows(WB), rows(DC),
                _mod_spec(mod, tm, 2), _mod_spec(mod, tm, 3), _mod_spec(mod, tm, 4), _mod_spec(mod, tm, 5),
                _const_spec((4, d)), _layer_spec((d, d), layer), _layer_spec((d, 2 * D_FF), layer),
                _const_spec((3, 2 * D_FF)), _const_spec((1, 2 * D_FF)), _layer_spec((D_FF, d), layer)]
    args = [x, oa, ob, oc, mod, mod, mod, mod, norms, wo, wup, cw, cb, wdn]
    if step_mode:
        in_specs.append(pl.BlockSpec((1, tm, 4 * D_FF), lambda b, i: (layer, i, 0)))
        args.append(state)
        ns_spec = pl.BlockSpec((1, tm, 4 * D_FF), lambda b, i: (b, i, 0))
        ns_shape = jax.ShapeDtypeStruct((bsz, length, 4 * D_FF), F32)
        scratch = [pltpu.VMEM((tm, D_FF), BF16)]
    else:
        ns_spec = pl.BlockSpec((1, 2, 2 * D_FF), lambda b, i: (b, 0, 0))
        ns_shape = jax.ShapeDtypeStruct((bsz, 2, 2 * D_FF), F32)
        scratch = [pltpu.VMEM((tm, D_FF), BF16), pltpu.VMEM((SUBLANES, 2 * D_FF), F32),
                   pltpu.VMEM((_FF_BUFS, tm + SUBLANES, _FF_CHUNK), F32)]
    return pl.pallas_call(
        functools.partial(_ffn_kernel, tm=tm, step_mode=step_mode),
        grid=(bsz, length // tm),
        in_specs=in_specs,
        out_specs=[rows(d), ns_spec],
        out_shape=[jax.ShapeDtypeStruct((bsz, length, d), F32), ns_shape],
        scratch_shapes=scratch,
        compiler_params=_params(("arbitrary", "arbitrary")),
        name="out_ffn_step" if step_mode else "out_ffn_seq",
    )(*args)


_SEQ_PER_STEP = 2


def _paged_kernel(pt_ref, q_ref, kn_ref, vn_ref, lfn_ref, *rest, n_pages, n_seq):
    n_blk = n_seq * n_pages
    kp = rest[0:n_blk]
    vp = rest[n_blk:2 * n_blk]
    lp = rest[2 * n_blk:3 * n_blk]
    o_ref = rest[3 * n_blk]
    pw = PAGE_SIZE * HA
    total = n_pages * pw
    past = n_pages * PAGE_SIZE

    rowh = lax.broadcasted_iota(jnp.int32, (SUBLANES, total), 0)
    lane = lax.broadcasted_iota(jnp.int32, (SUBLANES, total), 1)
    own = (lane % HA) == rowh
    pos = lax.broadcasted_iota(jnp.int32, (SUBLANES, past), 1)
    e_row = lax.broadcasted_iota(jnp.int32, (PAGE_SIZE, pw), 0)
    e_col = lax.broadcasted_iota(jnp.int32, (PAGE_SIZE, pw), 1)
    spread = jnp.where(e_col // HA == e_row, 1.0, 0.0).astype(BF16)

    for sq in range(n_seq):
        pages = range(sq * n_pages, (sq + 1) * n_pages)
        q = q_ref[sq]
        qb = q.astype(BF16)

        lf = jnp.concatenate([lp[j][0, 0] for j in pages], axis=1)
        lf = jnp.concatenate([lf, jnp.zeros_like(lf)], axis=0)
        suf = lf
        s = 1
        while s < past:
            suf = suf + jnp.where(pos + s < past, pltpu.roll(suf, past - s, 1), 0.0)
            s *= 2
        bias_hm = (suf - lf + lfn_ref[sq]) * LOG2E

        stacked = jnp.concatenate(
            [bias_hm[:, j * PAGE_SIZE:(j + 1) * PAGE_SIZE] for j in range(n_pages)], axis=0)
        hi = stacked.astype(BF16)
        r1 = stacked - hi.astype(F32)
        mid = r1.astype(BF16)
        lo = (r1 - mid.astype(F32)).astype(BF16)
        bias = _dot(hi, spread) + _dot(mid, spread) + _dot(lo, spread)

        logits = jnp.concatenate(
            [_dot_nt(qb, kp[j][0, 0].astype(BF16)) + bias[t * SUBLANES:(t + 1) * SUBLANES, :]
             for t, j in enumerate(pages)], axis=1)
        logits = jnp.where(own, logits, NEG)
        s_new = jnp.sum(q * kn_ref[sq], axis=-1, keepdims=True)
        m = jnp.maximum(jnp.max(logits, axis=-1, keepdims=True), s_new)
        p = jnp.exp2(logits - m)
        p_new = jnp.exp2(s_new - m)
        l = jnp.sum(p, axis=-1, keepdims=True) + p_new
        pb = p.astype(BF16)
        acc = p_new * vn_ref[sq]
        for t, j in enumerate(pages):
            acc = acc + _dot(pb[:, t * pw:(t + 1) * pw], vp[j][0, 0].astype(BF16))
        o_ref[sq] = acc / l


def _fox_sample(page_table, q, k_new, v_new, lf_new, pool_k, pool_v, pool_lf, layer):
    n, n_pages = page_table.shape
    pt = page_table.reshape(-1)
    pw = PAGE_SIZE * HA
    n_seq = _SEQ_PER_STEP if n % _SEQ_PER_STEP == 0 else 1
    row = pl.BlockSpec((n_seq, SUBLANES, DA), lambda i, pt: (i, 0, 0))

    def pages(width_shape):
        return [pl.BlockSpec((1, 1) + width_shape,
                             lambda i, pt, sq=sq, j=j: (layer, pt[(i * n_seq + sq) * n_pages + j], 0, 0))
                for sq in range(n_seq) for j in range(n_pages)]

    in_specs = ([row, row, row, pl.BlockSpec((n_seq, SUBLANES, 1), lambda i, pt: (i, 0, 0))]
                + pages((pw, DA)) + pages((pw, DA)) + pages((HA, PAGE_SIZE)))
    grid_spec = pltpu.PrefetchScalarGridSpec(
        num_scalar_prefetch=1, grid=(n // n_seq,), in_specs=in_specs,
        out_specs=pl.BlockSpec((n_seq, SUBLANES, DA), lambda i, pt: (i, 0, 0)))
    n_blk = n_seq * n_pages
    return pl.pallas_call(
        functools.partial(_paged_kernel, n_pages=n_pages, n_seq=n_seq),
        grid_spec=grid_spec,
        out_shape=jax.ShapeDtypeStruct((n, SUBLANES, DA), F32),
        compiler_params=_params(("arbitrary",)),
        name="fox_sample",
    )(pt, q, k_new, v_new, lf_new, *([pool_k] * n_blk), *([pool_v] * n_blk), *([pool_lf] * n_blk))


def _sret_kernel(q_ref, k_ref, v_ref, gate_ref, gn_ref, s0_ref, o_ref, sn_ref):
    q = q_ref[0]
    k = k_ref[0]
    v = v_ref[0]
    hw = DKB * DVB
    lane = lax.broadcasted_iota(jnp.int32, (DKB, hw), 1)
    rowi = lax.broadcasted_iota(jnp.int32, (DKB, hw), 0)
    rep = (lane // DVB == rowi).astype(BF16)
    til = (lane % DVB == rowi).astype(BF16)
    outs = []
    for h in range(HB):
        sl = slice(h * DKB, (h + 1) * DKB)
        g = math.exp(_LOG_G[h])
        sn = g * s0_ref[0, :, h * hw:(h + 1) * hw] + _dot(k[:, sl], rep) * _dot(v[:, sl], til)
        sn_ref[0, :, h * hw:(h + 1) * hw] = sn
        prod = _dot(q[:, sl], rep) * sn
        acc = prod[:, 0:LANES]
        for j in range(1, hw // LANES):
            acc = acc + prod[:, j * LANES:(j + 1) * LANES]
        oh = acc[:, 0:DVB] + acc[:, DVB:LANES]
        mu = jnp.mean(oh, axis=-1, keepdims=True)
        xc = oh - mu
        outs.append(xc * lax.rsqrt(jnp.mean(xc * xc, axis=-1, keepdims=True) + EPS))
    r = jnp.concatenate(outs, axis=1) * gn_ref[...]
    o_ref[0] = (r * gate_ref[0].astype(F32)).astype(BF16)


def _ret_sample(rq, rk, rv, gate, gn, state, layer, nb):
    _, n, _ = rq.shape
    sw = HB * DKB * DVB
    rows = pl.BlockSpec((1, nb, WB), lambda i: (0, i, 0))
    return pl.pallas_call(
        _sret_kernel,
        grid=(n // nb,),
        in_specs=[rows, rows, rows, rows, _const_spec((1, WB)),
                  pl.BlockSpec((1, nb, sw), lambda i: (layer, i, 0))],
        out_specs=[rows, pl.BlockSpec((1, nb, sw), lambda i: (0, i, 0))],
        out_shape=[jax.ShapeDtypeStruct((1, n, WB), BF16), jax.ShapeDtypeStruct((1, n, sw), F32)],
        compiler_params=_params(("arbitrary",)),
        name="ret_sample",
    )(rq, rk, rv, gate, gn, state)


def _sconv_kernel(ctx_ref, u_ref, w_ref, b_ref, lg_ref, lb_ref, o_ref, ns_ref):
    ctx = ctx_ref[0]
    u = u_ref[...]
    w = w_ref[...]
    cv = (jnp.sum(ctx * w[0:CONV_W - 1, :][None], axis=1, keepdims=True)
          + u * w[CONV_W - 1:CONV_W, :][None] + b_ref[...][None])
    o_ref[...] = _silu(_layernorm(cv, lg_ref[...][None], lb_ref[...][None])).astype(BF16)
    ns_ref[0, :, 0:CONV_W - 2, :] = ctx[:, 1:CONV_W - 1, :]
    ns_ref[0, :, CONV_W - 2:CONV_W - 1, :] = u


def _conv_sample(state, u, w, b, lg, lb, layer, nb):
    n = u.shape[0]
    return pl.pallas_call(
        _sconv_kernel,
        grid=(n // nb,),
        in_specs=[pl.BlockSpec((1, nb, CONV_W - 1, DC), lambda i: (layer, i, 0, 0)),
                  pl.BlockSpec((nb, 1, DC), lambda i: (i, 0, 0)),
                  _const_spec((CONV_W, DC)), _const_spec((1, DC)), _const_spec((1, DC)), _const_spec((1, DC))],
        out_specs=[pl.BlockSpec((nb, 1, DC), lambda i: (i, 0, 0)),
                   pl.BlockSpec((1, nb, CONV_W - 1, DC), lambda i: (0, i, 0, 0))],
        out_shape=[jax.ShapeDtypeStruct((n, 1, DC), BF16),
                   jax.ShapeDtypeStruct((1, n, CONV_W - 1, DC), F32)],
        compiler_params=_params(("arbitrary",)),
        name="conv_sample",
    )(state, u, w, b, lg, lb)


def _reorder_w_in(w):
    wt = w.T
    o = 3 * WA
    tail = jnp.pad(wt[o:o + HA], ((0, LANES - HA), (0, 0)))
    return jnp.concatenate([wt[:o], wt[o + HA:], tail], axis=0).astype(BF16)


def _pick(total, pref):
    t = min(total, pref)
    assert total % t == 0
    return t


def kernel(x_prompt, x_sample, cache_k, cache_v, cache_logf, state_ret, state_conv, state_ffn_conv, page_table,
           c_prompt, c_sample, ada_w, ada_b, norms, w_in, b_f, ret_gn_g, conv_w, conv_b, conv_ln_g, conv_ln_b,
           w_o, ffn_up, ffn_conv_w, ffn_conv_b, ffn_down):
    depth = ada_w.shape[0]
    bp, sp, d = x_prompt.shape
    ns, ts, _ = x_sample.shape
    assert ts == 1 and d == D_MODEL
    n_pool = cache_k.shape[1]
    n_pages = page_table.shape[1]

    n_c = ns + bp
    n_c_pad = -(-n_c // SUBLANES) * SUBLANES
    c_all = jnp.pad(jnp.concatenate([c_sample, c_prompt], axis=0), ((0, n_c_pad - n_c), (0, 0)))
    mod_all = _ada_mod(c_all, ada_w, ada_b)

    cos_p, sin_p = _rope_tables(sp, 0, 1)
    cos_s, sin_s = _rope_tables(ns, n_pages * PAGE_SIZE, 0)

    pool_k = cache_k.reshape(depth, n_pool, PAGE_SIZE * HA, DA)
    pool_v = cache_v.reshape(depth, n_pool, PAGE_SIZE * HA, DA)
    pool_lf = cache_logf.transpose(0, 1, 3, 2)
    st_ret = state_ret.reshape(depth, ns, HB * DKB * DVB)
    st_ffn = state_ffn_conv.reshape(depth, ns, 4 * D_FF)

    tm_p = _pick(sp, 512)
    tq = _pick(sp, 1024)
    chunk = _pick(sp, 128)
    nb_s = _pick(ns, 32)

    xp = x_prompt
    xs = x_sample.reshape(1, ns, d)
    p_out, s_out = [], []
    kv_p = kv_s = None
    wo, wup, wdn = w_o.astype(BF16), ffn_up.astype(BF16), ffn_down.astype(BF16)

    def heads_on_rows(a):
        return jnp.pad(a.reshape(ns, HA, DA), ((0, 0), (0, SUBLANES - HA), (0, 0)))

    for l in range(depth):
        w_re = _reorder_w_in(w_in[l])
        bf_pad = jnp.pad(b_f[l], (0, LANES - HA)).reshape(1, LANES)
        n0 = norms[l, 0:1]
        gn = ret_gn_g[l].reshape(1, WB)
        cb = conv_b[l].reshape(1, DC)
        clg = conv_ln_g[l].reshape(1, DC)
        clb = conv_ln_b[l].reshape(1, DC)
        fcb = ffn_conv_b[l].reshape(1, 2 * D_FF)
        mod_s = mod_all[l, 0:ns].reshape(1, ns, 6 * d)
        mod_p = mod_all[l, ns:ns + bp].reshape(bp, 1, 6 * d)

        ko, vo, qa, ka, va, lft, rq, rk, rv, gate, oc, conv_st = _in_proj(
            xp, mod_p, n0, w_re, bf_pad, cos_p, sin_p, tm_p, depth, l, kv_p, conv=(conv_w[l], cb, clg, clb))
        kv_p = (ko, vo)
        oa = _fox_prompt(qa, ka, va, _cumsum(lft), tq)
        ob, s_bd = _ret_prompt(rq, rk, rv, gate, gn, chunk)
        xp, ffn_st = _out_ffn(xp, oa, ob, oc, mod_p, norms[l], wo, wup, ffn_conv_w[l], fcb, wdn, tm_p, layer=l)
        s5 = s_bd.reshape(bp, HB, DKB, HB, DVB)
        ret_st = jnp.stack([s5[:, h, :, h, :] for h in range(HB)], axis=1)
        p_out.append((lft[:, 0:HA, :].transpose(0, 2, 1), ret_st, conv_st, ffn_st))

        ko, vo, qa, ka, va, lft, rq, rk, rv, gate, u = _in_proj(
            xs, mod_s, n0, w_re, bf_pad, cos_s, sin_s, ns, depth, l, kv_s)
        kv_s = (ko, vo)
        lfn = lft[0].T.reshape(ns, SUBLANES, 1)
        oa = _fox_sample(page_table, heads_on_rows(qa.astype(F32)), heads_on_rows(ko[l, 0]),
                         heads_on_rows(vo[l, 0]), lfn, pool_k, pool_v, pool_lf, l)
        oa = oa[:, 0:HA, :].reshape(1, ns, WA).astype(BF16)
        ob, ret_new = _ret_sample(rq, rk, rv, gate, gn, st_ret, l, nb_s)
        oc, conv_new = _conv_sample(state_conv, u.reshape(ns, 1, DC), conv_w[l], cb, clg, clb, l, nb_s)
        xs, ffn_new = _out_ffn(xs, oa, ob, oc.reshape(1, ns, DC), mod_s, norms[l],
                               wo, wup, ffn_conv_w[l], fcb, wdn, ns, state=st_ffn, layer=l)
        s_out.append((lft[0, 0:HA, :].T.reshape(ns, 1, HA), ret_new.reshape(ns, HB, DKB, DVB),
                      conv_new[0], ffn_new.reshape(ns, 2, 2 * D_FF)))

    plf, pret, pconv, pffn = [jnp.stack(a) for a in zip(*p_out)]
    slf, sret, sconv, sffn = [jnp.stack(a) for a in zip(*s_out)]
    pk, pv = [a.reshape(depth, bp, sp, HA, DA) for a in kv_p]
    sk, sv = [a.reshape(depth, ns, 1, HA, DA) for a in kv_s]
    return (xp, xs.reshape(ns, 1, d), pk, pv, plf, pret, pconv, pffn, sk, sv, slf, sret, sconv, sffn)
```

```python
import functools
import math

import jax
import jax.numpy as jnp
import numpy as np
from jax import lax
from jax.experimental import pallas as pl
from jax.experimental.pallas import tpu as pltpu

F32 = jnp.float32
BF16 = jnp.bfloat16

D_MODEL = 1024
HA, DA = 4, 128
WA = HA * DA
HB, DKB, DVB = 4, 64, 64
WB = HB * DVB
DC = 256
CONV_W = 31
D_FF = 2816
PAGE_SIZE = 128
ROPE_BASE = 10000.0
EPS = 1e-6
NEG = -1e30
LOG2E = math.log2(math.e)

LANES = 128
SUBLANES = 8
VMEM_LIMIT = 56 * 1024 * 1024

_OQ, _OK, _OV = 0, 512, 1024
_OBQ, _OBK, _OBV, _OBG = 1536, 1792, 2048, 2304
_OCA, _OCB, _OAF = 2560, 2816, 3072
N_IN_PAD = 3200

_LOG_G = [math.log1p(-(2.0 ** (-5.0 - h))) for h in range(HB)]


def _sigmoid(x):
    return 1.0 / (1.0 + jnp.exp(-x))


def _silu(x):
    return x * _sigmoid(x)


def _log_sigmoid(x):
    return jnp.minimum(x, 0.0) - jnp.log1p(jnp.exp(-jnp.abs(x)))


def _rms(x):
    return x * lax.rsqrt(jnp.mean(x * x, axis=-1, keepdims=True) + EPS)


def _dot(a, b):
    return jnp.dot(a, b, preferred_element_type=F32)


def _dot_nt(a, b):
    return lax.dot_general(a, b, (((1,), (1,)), ((), ())), preferred_element_type=F32)


def _params(sem):
    return pltpu.CompilerParams(dimension_semantics=sem, vmem_limit_bytes=VMEM_LIMIT)


def _const_spec(shape):
    nd = len(shape)
    return pl.BlockSpec(shape, lambda *_: (0,) * nd, pipeline_mode=pl.Buffered(1))


def _layer_spec(shape, layer):
    nd = len(shape)
    return pl.BlockSpec((1,) + tuple(shape), lambda *_: (layer,) + (0,) * nd, pipeline_mode=pl.Buffered(1))


def _ada_kernel(c_ref, w_ref, b_ref, o_ref):
    s = _silu(c_ref[...]).astype(BF16)
    o_ref[0] = _dot(s, w_ref[0].astype(BF16)) + b_ref[0]


def _ada_mod(c_all, ada_w, ada_b):
    depth, d, n6 = ada_w.shape
    r = c_all.shape[0]
    tn = 1536
    return pl.pallas_call(
        _ada_kernel,
        grid=(depth, n6 // tn),
        in_specs=[pl.BlockSpec((r, d), lambda l, j: (0, 0)),
                  pl.BlockSpec((1, d, tn), lambda l, j: (l, 0, j)),
                  pl.BlockSpec((1, 1, tn), lambda l, j: (l, 0, j))],
        out_specs=pl.BlockSpec((1, r, tn), lambda l, j: (l, 0, j)),
        out_shape=jax.ShapeDtypeStruct((depth, r, n6), F32),
        compiler_params=_params(("arbitrary", "arbitrary")),
        name="ada_mod",
    )(c_all, ada_w, ada_b.reshape(depth, 1, n6))


def _rope_kernel(inv_ref, cos_ref, sin_ref, *, pos0, stride, tm):
    i = pl.program_id(0)
    row = lax.broadcasted_iota(jnp.int32, (tm, LANES), 0) + i * tm
    lane = lax.broadcasted_iota(jnp.int32, (tm, LANES), 1)
    ang = (pos0 + stride * row).astype(F32) * inv_ref[...]
    cos_ref[...] = jnp.cos(ang)
    s = jnp.sin(ang)
    sin_ref[...] = jnp.where((lane % DKB) < DKB // 2, -s, s)


def _rope_tables(length, pos0, stride):
    half = DKB // 2
    inv = ROPE_BASE ** (-jnp.arange(half, dtype=F32) / half)
    inv = jnp.tile(inv, LANES // half).reshape(1, LANES)
    tm = min(length, 1024)
    out = jax.ShapeDtypeStruct((length, LANES), F32)
    return pl.pallas_call(
        functools.partial(_rope_kernel, pos0=pos0, stride=stride, tm=tm),
        grid=(length // tm,),
        in_specs=[pl.BlockSpec((1, LANES), lambda i: (0, 0))],
        out_specs=[pl.BlockSpec((tm, LANES), lambda i: (i, 0))] * 2,
        out_shape=[out, out],
        compiler_params=_params(("arbitrary",)),
        name="rope_tables",
    )(inv)


def _in_kernel(*refs, tm, aliased, conv):
    n_in = 12 if conv else 8
    ins, rest = refs[:n_in], refs[n_in + (2 if aliased else 0):]
    x_ref, sh_ref, sc_ref, g_ref, w_ref, bf_ref, cos_ref, sin_ref = ins[:8]
    ko_ref, vo_ref, qa_ref, ka_ref, va_ref, lft_ref, rq_ref, rk_ref, rv_ref, gate_ref, u_ref = rest[:11]
    if conv:
        _conv_reset(rest[12])
    x = x_ref[0]
    h = _rms(x) * g_ref[...]
    hb = (h * (1.0 + sc_ref[0]) + sh_ref[0]).astype(BF16)

    def proj(off, width):
        return _dot_nt(hb, w_ref[off:off + width, :])

    u = proj(_OCA, DC) * _sigmoid(proj(_OCB, DC))
    if conv:
        _conv_tile(u, *ins[8:12], u_ref, rest[11], rest[12], tm)
    else:
        u_ref[0] = u

    qa_ref[0] = (proj(_OQ, WA) * (DA ** -0.5 * LOG2E)).astype(BF16)
    k = proj(_OK, WA)
    ka_ref[0] = k.astype(BF16)
    v = proj(_OV, WA)
    va_ref[0] = v.astype(BF16)
    for hd in range(HA):
        ko_ref[0, 0, pl.ds(hd, tm, stride=HA), :] = k[:, hd * DA:(hd + 1) * DA]
        vo_ref[0, 0, pl.ds(hd, tm, stride=HA), :] = v[:, hd * DA:(hd + 1) * DA]

    lf = _log_sigmoid(proj(_OAF, LANES) + bf_ref[...])
    lft_ref[0] = lf.T[0:SUBLANES, :]

    cos = cos_ref[...]
    sin = sin_ref[...]
    lane = lax.broadcasted_iota(jnp.int32, cos.shape, 1)
    first = (lane % DKB) < DKB // 2

    def rope(t):
        partner = jnp.where(first, pltpu.roll(t, LANES - DKB // 2, 1), pltpu.roll(t, DKB // 2, 1))
        return t * cos + partner * sin

    bq = proj(_OBQ, WB)
    bk = proj(_OBK, WB)
    for j in range(WB // LANES):
        sl = slice(j * LANES, (j + 1) * LANES)
        rq_ref[0, :, sl] = rope(bq[:, sl]).astype(BF16)
        rk_ref[0, :, sl] = (rope(bk[:, sl]) * (DKB ** -0.5)).astype(BF16)
    rv_ref[0] = proj(_OBV, WB).astype(BF16)
    gate_ref[0] = _silu(proj(_OBG, WB)).astype(BF16)


def _mod_spec(mod, tm, col):
    rows = mod.shape[1]
    if rows == 1:
        return pl.BlockSpec((1, 1, D_MODEL), lambda b, i: (b, 0, col))
    return pl.BlockSpec((1, tm, D_MODEL), lambda b, i: (b, i, col))


def _in_proj(x, mod, norm_g, w_re, bf_pad, cos, sin, tm, depth, layer, kv_prev, conv=None):
    bsz, length, d = x.shape
    grid = (bsz, length // tm)

    def rows(width):
        return pl.BlockSpec((1, tm, width), lambda b, i: (b, i, 0))

    def shp(width, dt):
        return jax.ShapeDtypeStruct((bsz, length, width), dt)

    tab = pl.BlockSpec((tm, LANES), lambda b, i: (i, 0))
    cache_spec = pl.BlockSpec((1, 1, tm * HA, DA), lambda b, i: (layer, b, i, 0))
    cache_shape = jax.ShapeDtypeStruct((depth, bsz, length * HA, DA), F32)
    in_specs = [rows(d), _mod_spec(mod, tm, 0), _mod_spec(mod, tm, 1),
                _const_spec((1, d)), _const_spec((N_IN_PAD, d)), _const_spec((1, LANES)), tab, tab]
    args = [x, mod, mod, norm_g, w_re, bf_pad, cos, sin]
    out_specs = [cache_spec, cache_spec, rows(WA), rows(WA), rows(WA),
                 pl.BlockSpec((1, SUBLANES, tm), lambda b, i: (b, 0, i)),
                 rows(WB), rows(WB), rows(WB), rows(WB), rows(DC)]
    out_shape = [cache_shape, cache_shape, shp(WA, BF16), shp(WA, BF16), shp(WA, BF16),
                 jax.ShapeDtypeStruct((bsz, SUBLANES, length), F32),
                 shp(WB, BF16), shp(WB, BF16), shp(WB, BF16), shp(WB, BF16),
                 shp(DC, F32 if conv is None else BF16)]
    scratch = []
    if conv is not None:
        in_specs += [_const_spec((CONV_W, DC))] + [_const_spec((1, DC))] * 3
        args += list(conv)
        out_specs.append(pl.BlockSpec((1, CONV_W - 1, DC), lambda b, i: (b, 0, 0)))
        out_shape.append(jax.ShapeDtypeStruct((bsz, CONV_W - 1, DC), F32))
        scratch = [pltpu.VMEM((SUBLANES, tm + _CONV_PAD, DC), F32)]
    aliases = {}
    if kv_prev is not None:
        aliases = {len(args): 0, len(args) + 1: 1}
        in_specs += [pl.BlockSpec(memory_space=pl.ANY)] * 2
        args += list(kv_prev)
    return pl.pallas_call(
        functools.partial(_in_kernel, tm=tm, aliased=kv_prev is not None, conv=conv is not None),
        grid=grid,
        in_specs=in_specs,
        out_specs=out_specs,
        out_shape=out_shape,
        scratch_shapes=scratch,
        input_output_aliases=aliases,
        compiler_params=_params(("arbitrary", "arbitrary")),
        name="in_proj",
    )(*args)


def _cumsum_kernel(x_ref, o_ref):
    x = x_ref[0]
    length = x.shape[1]
    lane = lax.broadcasted_iota(jnp.int32, x.shape, 1)
    s = 1
    while s < length:
        x = x + jnp.where(lane >= s, pltpu.roll(x, s, 1), 0.0)
        s *= 2
    o_ref[0] = x * LOG2E


def _cumsum(lft):
    bsz, r, length = lft.shape
    return pl.pallas_call(
        _cumsum_kernel,
        grid=(bsz,),
        in_specs=[pl.BlockSpec((1, r, length), lambda b: (b, 0, 0))],
        out_specs=pl.BlockSpec((1, r, length), lambda b: (b, 0, 0)),
        out_shape=jax.ShapeDtypeStruct(lft.shape, F32),
        compiler_params=_params(("arbitrary",)),
        name="logf_cumsum",
    )(lft)


def _fox_kernel(q_ref, k_ref, v_ref, ck_ref, o_ref, *, tq):
    qi = pl.program_id(1)
    half = tq // 2
    ones = jnp.ones((tq, DA), BF16)

    def update(m, acc, h, q_rows, off, n_keys, ck, mask):
        sl = slice(h * DA, (h + 1) * DA)
        k = k_ref[0, pl.ds(off, n_keys), sl]
        v1 = jnp.concatenate([v_ref[0, pl.ds(off, n_keys), sl], ones[0:n_keys]], axis=1)
        s = _dot_nt(q_ref[0, q_rows, sl], k) - ck
        if mask is not None:
            s = jnp.where(mask, s, NEG)
        m_new = jnp.maximum(m, jnp.max(s, axis=-1, keepdims=True))
        p = jnp.exp2(s - m_new)
        return m_new, jnp.exp2(m - m_new) * acc + _dot(p.astype(BF16), v1)

    def step(kb, carry):
        off = pl.multiple_of(kb * tq, tq)
        return tuple(update(*carry[h], h, slice(0, tq), off, tq, ck_ref[0, kb, h:h + 1, :], None)
                     for h in range(HA))

    def diagonal(carry):
        off = pl.multiple_of(qi * tq, tq)
        row = lax.broadcasted_iota(jnp.int32, (half, tq), 0)
        col = lax.broadcasted_iota(jnp.int32, (half, tq), 1)
        row_a = lax.broadcasted_iota(jnp.int32, (half, half), 0)
        col_a = lax.broadcasted_iota(jnp.int32, (half, half), 1)
        out = []
        for h in range(HA):
            m, acc = carry[h]
            m_a, acc_a = update(m[0:half], acc[0:half], h, slice(0, half), off, half,
                                ck_ref[0, qi, h:h + 1, 0:half], row_a >= col_a)
            m_b, acc_b = update(m[half:tq], acc[half:tq], h, slice(half, tq), off, tq,
                                ck_ref[0, qi, h:h + 1, :], row + half >= col)
            out.append((jnp.concatenate([m_a, m_b], axis=0), jnp.concatenate([acc_a, acc_b], axis=0)))
        return tuple(out)

    init = tuple((jnp.full((tq, 1), NEG, F32), jnp.zeros((tq, 2 * DA), F32)) for _ in range(HA))
    carry = diagonal(lax.fori_loop(0, qi, step, init))
    for h in range(HA):
        acc = carry[h][1]
        o_ref[0, :, h * DA:(h + 1) * DA] = (acc[:, 0:DA] / acc[:, DA:2 * DA]).astype(BF16)


def _fox_prompt(qa, ka, va, cum, tq):
    bsz, length, _ = qa.shape
    nb = length // tq
    ck = cum.reshape(bsz, SUBLANES, nb, tq).transpose(0, 2, 1, 3)
    full = pl.BlockSpec((1, length, WA), lambda b, i: (b, 0, 0), pipeline_mode=pl.Buffered(1))
    return pl.pallas_call(
        functools.partial(_fox_kernel, tq=tq),
        grid=(bsz, nb),
        in_specs=[pl.BlockSpec((1, tq, WA), lambda b, i: (b, i, 0)), full, full,
                  pl.BlockSpec((1, nb, SUBLANES, tq), lambda b, i: (b, 0, 0, 0))],
        out_specs=pl.BlockSpec((1, tq, WA), lambda b, i: (b, i, 0)),
        out_shape=jax.ShapeDtypeStruct((bsz, length, WA), BF16),
        compiler_params=_params(("arbitrary", "arbitrary")),
        name="fox_prompt",
    )(qa, ka, va, ck)


def _head_mean(z, avg):
    hi = z.astype(BF16)
    lo = (z - hi.astype(F32)).astype(BF16)
    return _dot(hi, avg) + _dot(lo, avg)


def _group_avg_matrix():
    r = lax.broadcasted_iota(jnp.int32, (WB, WB), 0) // DVB
    c = lax.broadcasted_iota(jnp.int32, (WB, WB), 1) // DVB
    return r == c


def _lane_log_g(shape, axis):
    head = lax.broadcasted_iota(jnp.int32, shape, axis) // DVB
    lg = jnp.full(shape, _LOG_G[HB - 1], F32)
    for h in range(HB - 1):
        lg = jnp.where(head == h, _LOG_G[h], lg)
    return lg


def _ret_kernel(q_ref, k_ref, v_ref, gate_ref, gn_ref, o_ref, so_ref,
                s_ref, dec_ref, xi_ref, zeta_ref, gc_ref, *, chunk, bsz):
    ci = pl.program_id(0)
    same_head = _group_avg_matrix()

    @pl.when(ci == 0)
    def _():
        s_ref[...] = jnp.zeros_like(s_ref)
        r_i = lax.broadcasted_iota(jnp.int32, (chunk, chunk), 0)
        c_i = lax.broadcasted_iota(jnp.int32, (chunk, chunk), 1)
        rel = (r_i - c_i).astype(F32)
        for h in range(HB):
            dec_ref[h] = jnp.where(rel >= 0, jnp.exp(_LOG_G[h] * jnp.maximum(rel, 0.0)), 0.0)
        lg = _lane_log_g((chunk, WB), 1)
        idx = lax.broadcasted_iota(jnp.int32, (chunk, WB), 0).astype(F32)
        xi_ref[...] = jnp.exp(lg * (idx + 1.0))
        zeta_ref[...] = jnp.exp(lg * (chunk - 1.0 - idx))
        gc_ref[...] = jnp.exp(_lane_log_g((WB, WB), 0) * float(chunk))

    head = lax.broadcasted_iota(jnp.int32, (1, WB), 1) // DVB
    avg = jnp.where(same_head, 1.0 / DVB, 0.0).astype(BF16)
    for b in range(bsz):
        q = q_ref[b]
        k = k_ref[b]
        v = v_ref[b]
        o = jnp.zeros((chunk, WB), F32)
        for h in range(HB):
            mh = head == h
            att = _dot_nt(jnp.where(mh, q, jnp.zeros_like(q)), k) * dec_ref[h]
            o = o + jnp.where(mh, _dot(att.astype(BF16), v), 0.0)
        s_old = s_ref[b]
        o = o + _dot(q, s_old.astype(BF16)) * xi_ref[...]
        kz_t = (k.astype(F32) * zeta_ref[...]).T.astype(BF16)
        s_ref[b] = jnp.where(same_head, gc_ref[...] * s_old + _dot(kz_t, v), 0.0)

        xc = o - _head_mean(o, avg)
        var = _head_mean(xc * xc, avg)
        r = xc * lax.rsqrt(var + EPS) * gn_ref[...]
        o_ref[b] = (r * gate_ref[b].astype(F32)).astype(BF16)

    @pl.when(ci == pl.num_programs(0) - 1)
    def _():
        so_ref[...] = s_ref[...]


def _ret_prompt(rq, rk, rv, gate, gn, chunk):
    bsz, length, _ = rq.shape
    rows = pl.BlockSpec((bsz, chunk, WB), lambda i: (0, i, 0))
    return pl.pallas_call(
        functools.partial(_ret_kernel, chunk=chunk, bsz=bsz),
        grid=(length // chunk,),
        in_specs=[rows, rows, rows, rows, _const_spec((1, WB))],
        out_specs=[rows, pl.BlockSpec((bsz, WB, WB), lambda i: (0, 0, 0))],
        out_shape=[jax.ShapeDtypeStruct((bsz, length, WB), BF16),
                   jax.ShapeDtypeStruct((bsz, WB, WB), F32)],
        scratch_shapes=[pltpu.VMEM((bsz, WB, WB), F32), pltpu.VMEM((HB, chunk, chunk), F32),
                        pltpu.VMEM((chunk, WB), F32), pltpu.VMEM((chunk, WB), F32), pltpu.VMEM((WB, WB), F32)],
        compiler_params=_params(("arbitrary",)),
        name="ret_prompt",
    )(rq, rk, rv, gate, gn)


_CONV_PAD = 32


def _layernorm(x, g, b):
    mu = jnp.mean(x, axis=-1, keepdims=True)
    xc = x - mu
    return xc * lax.rsqrt(jnp.mean(xc * xc, axis=-1, keepdims=True) + EPS) * g + b


_CONV_ROWS = 128


def _conv_reset(ext_ref):
    @pl.when(pl.program_id(1) == 0)
    def _():
        ext_ref[0, 0:_CONV_PAD, :] = jnp.zeros((_CONV_PAD, DC), F32)


def _conv_tile(u, w_ref, b_ref, lg_ref, lb_ref, o_ref, st_ref, ext_ref, tm):
    ext_ref[0, _CONV_PAD:_CONV_PAD + tm, :] = u
    span = tm + _CONV_PAD - SUBLANES
    for r in range(1, SUBLANES):
        ext_ref[r, 0:span, :] = ext_ref[0, r:r + span, :]

    base = _CONV_PAD - (CONV_W - 1)
    for rb in range(tm // _CONV_ROWS):
        acc = jnp.zeros((_CONV_ROWS, DC), F32) + b_ref[...]
        for j in range(CONV_W):
            a, r = divmod(base + j, SUBLANES)
            start = a * SUBLANES + rb * _CONV_ROWS
            acc = acc + w_ref[j:j + 1, :] * ext_ref[r, start:start + _CONV_ROWS, :]
        rows = slice(rb * _CONV_ROWS, (rb + 1) * _CONV_ROWS)
        o_ref[0, rows, :] = _silu(_layernorm(acc, lg_ref[...], lb_ref[...])).astype(BF16)

    st_ref[0] = ext_ref[0, tm + base:tm + _CONV_PAD, :]
    ext_ref[0, 0:_CONV_PAD, :] = ext_ref[0, tm:tm + _CONV_PAD, :]


_FF_CHUNK = 256
_FF_BUFS = 4
_FF_GROUP = D_FF // _FF_CHUNK


def _ffn_kernel(*refs, tm, step_mode):
    if step_mode:
        (x_ref, oa_ref, ob_ref, oc_ref, g1_ref, sh2_ref, sc2_ref, g2_ref, n_ref,
         wo_ref, wup_ref, cw_ref, cb_ref, wdn_ref, st_ref, y_ref, ns_ref, act_ref) = refs
    else:
        (x_ref, oa_ref, ob_ref, oc_ref, g1_ref, sh2_ref, sc2_ref, g2_ref, n_ref,
         wo_ref, wup_ref, cw_ref, cb_ref, wdn_ref, y_ref, ns_ref, act_ref, ctx_ref, bufs_ref) = refs
        i = pl.program_id(1)

        @pl.when(i == 0)
        def _():
            ctx_ref[...] = jnp.zeros_like(ctx_ref)

    wo_ref, wup_ref, wdn_ref = wo_ref.at[0], wup_ref.at[0], wdn_ref.at[0]
    mix = (_dot(oa_ref[0], wo_ref[0:WA, :]) + _dot(ob_ref[0], wo_ref[WA:WA + WB, :])
           + _dot(oc_ref[0], wo_ref[WA + WB:, :]))
    x1 = x_ref[0] + g1_ref[0] * (_rms(mix) * n_ref[1:2, :])
    h2 = ((_rms(x1) * n_ref[2:3, :]) * (1.0 + sc2_ref[0]) + sh2_ref[0]).astype(BF16)

    n_chunks = D_FF // _FF_CHUNK
    f = None
    for c in range(n_chunks):
        halves = []
        for half in range(2):
            c0 = half * D_FF + c * _FF_CHUNK
            cs = slice(c0, c0 + _FF_CHUNK)
            up = _dot(h2, wup_ref[:, cs])
            if step_mode:
                prev2 = st_ref[0, :, cs]
                prev1 = st_ref[0, :, 2 * D_FF + c0:2 * D_FF + c0 + _FF_CHUNK]
                ns_ref[0, :, cs] = prev1
                ns_ref[0, :, 2 * D_FF + c0:2 * D_FF + c0 + _FF_CHUNK] = up
            else:
                buf_ref = bufs_ref.at[(2 * c + half) % _FF_BUFS]
                buf_ref[0:SUBLANES, :] = ctx_ref[:, cs]
                buf_ref[SUBLANES:SUBLANES + tm, :] = up
                prev1 = buf_ref[SUBLANES - 1:SUBLANES - 1 + tm, :]
                prev2 = buf_ref[SUBLANES - 2:SUBLANES - 2 + tm, :]
                ctx_ref[:, cs] = buf_ref[tm:tm + SUBLANES, :]
            halves.append(cw_ref[0:1, cs] * prev2 + cw_ref[1:2, cs] * prev1
                          + cw_ref[2:3, cs] * up + cb_ref[:, cs])
        act_ref[:, c * _FF_CHUNK:(c + 1) * _FF_CHUNK] = (_silu(halves[0]) * halves[1]).astype(BF16)
        if (c + 1) % _FF_GROUP == 0 or c + 1 == n_chunks:
            lo = (c // _FF_GROUP) * _FF_GROUP * _FF_CHUNK
            part = _dot(act_ref[:, lo:(c + 1) * _FF_CHUNK], wdn_ref[lo:(c + 1) * _FF_CHUNK, :])
            f = part if f is None else f + part
    y_ref[0] = x1 + g2_ref[0] * (_rms(f) * n_ref[3:4, :])

    if not step_mode:
        @pl.when(i == pl.num_programs(1) - 1)
        def _():
            ns_ref[0] = ctx_ref[SUBLANES - 2:SUBLANES, :]


def _out_ffn(x, oa, ob, oc, mod, norms, wo, wup, cw, cb, wdn, tm, state=None, layer=0):
    bsz, length, d = x.shape
    step_mode = state is not None

    def rows(width):
        return pl.BlockSpec((1, tm, width), lambda b, i: (b, i, 0))

    in_specs = [rows(d), rows(WA), rows(WB), rows(DC),
                _mod_spec(mod, tm, 2), _mod_spec(mod, tm, 3), _mod_spec(mod, tm, 4), _mod_spec(mod, tm, 5),
                _const_spec((4, d)), _layer_spec((d, d), layer), _layer_spec((d, 2 * D_FF), layer),
                _const_spec((3, 2 * D_FF)), _const_spec((1, 2 * D_FF)), _layer_spec((D_FF, d), layer)]
    args = [x, oa, ob, oc, mod, mod, mod, mod, norms, wo, wup, cw, cb, wdn]
    if step_mode:
        in_specs.append(pl.BlockSpec((1, tm, 4 * D_FF), lambda b, i: (layer, i, 0)))
        args.append(state)
        ns_spec = pl.BlockSpec((1, tm, 4 * D_FF), lambda b, i: (b, i, 0))
        ns_shape = jax.ShapeDtypeStruct((bsz, length, 4 * D_FF), F32)
        scratch = [pltpu.VMEM((tm, D_FF), BF16)]
    else:
        ns_spec = pl.BlockSpec((1, 2, 2 * D_FF), lambda b, i: (b, 0, 0))
        ns_shape = jax.ShapeDtypeStruct((bsz, 2, 2 * D_FF), F32)
        scratch = [pltpu.VMEM((tm, D_FF), BF16), pltpu.VMEM((SUBLANES, 2 * D_FF), F32),
                   pltpu.VMEM((_FF_BUFS, tm + SUBLANES, _FF_CHUNK), F32)]
    return pl.pallas_call(
        functools.partial(_ffn_kernel, tm=tm, step_mode=step_mode),
        grid=(bsz, length // tm),
        in_specs=in_specs,
        out_specs=[rows(d), ns_spec],
        out_shape=[jax.ShapeDtypeStruct((bsz, length, d), F32), ns_shape],
        scratch_shapes=scratch,
        compiler_params=_params(("arbitrary", "arbitrary")),
        name="out_ffn_step" if step_mode else "out_ffn_seq",
    )(*args)


_SEQ_PER_STEP = 2


def _paged_kernel(pt_ref, q_ref, kn_ref, vn_ref, lfn_ref, *rest, n_pages, n_seq):
    n_blk = n_seq * n_pages
    kp = rest[0:n_blk]
    vp = rest[n_blk:2 * n_blk]
    lp = rest[2 * n_blk:3 * n_blk]
    o_ref = rest[3 * n_blk]
    pw = PAGE_SIZE * HA
    total = n_pages * pw
    past = n_pages * PAGE_SIZE

    rowh = lax.broadcasted_iota(jnp.int32, (SUBLANES, total), 0)
    lane = lax.broadcasted_iota(jnp.int32, (SUBLANES, total), 1)
    own = (lane % HA) == rowh
    pos = lax.broadcasted_iota(jnp.int32, (SUBLANES, past), 1)
    e_row = lax.broadcasted_iota(jnp.int32, (PAGE_SIZE, pw), 0)
    e_col = lax.broadcasted_iota(jnp.int32, (PAGE_SIZE, pw), 1)
    spread = jnp.where(e_col // HA == e_row, 1.0, 0.0).astype(BF16)

    for sq in range(n_seq):
        pages = range(sq * n_pages, (sq + 1) * n_pages)
        q = q_ref[sq]
        qb = q.astype(BF16)

        lf = jnp.concatenate([lp[j][0, 0] for j in pages], axis=1)
        lf = jnp.concatenate([lf, jnp.zeros_like(lf)], axis=0)
        suf = lf
        s = 1
        while s < past:
            suf = suf + jnp.where(pos + s < past, pltpu.roll(suf, past - s, 1), 0.0)
            s *= 2
        bias_hm = (suf - lf + lfn_ref[sq]) * LOG2E

        stacked = jnp.concatenate(
            [bias_hm[:, j * PAGE_SIZE:(j + 1) * PAGE_SIZE] for j in range(n_pages)], axis=0)
        hi = stacked.astype(BF16)
        r1 = stacked - hi.astype(F32)
        mid = r1.astype(BF16)
        lo = (r1 - mid.astype(F32)).astype(BF16)
        bias = _dot(hi, spread) + _dot(mid, spread) + _dot(lo, spread)

        logits = jnp.concatenate(
            [_dot_nt(qb, kp[j][0, 0].astype(BF16)) + bias[t * SUBLANES:(t + 1) * SUBLANES, :]
             for t, j in enumerate(pages)], axis=1)
        logits = jnp.where(own, logits, NEG)
        s_new = jnp.sum(q * kn_ref[sq], axis=-1, keepdims=True)
        m = jnp.maximum(jnp.max(logits, axis=-1, keepdims=True), s_new)
        p = jnp.exp2(logits - m)
        p_new = jnp.exp2(s_new - m)
        l = jnp.sum(p, axis=-1, keepdims=True) + p_new
        pb = p.astype(BF16)
        acc = p_new * vn_ref[sq]
        for t, j in enumerate(pages):
            acc = acc + _dot(pb[:, t * pw:(t + 1) * pw], vp[j][0, 0].astype(BF16))
        o_ref[sq] = acc / l


def _fox_sample(page_table, q, k_new, v_new, lf_new, pool_k, pool_v, pool_lf, layer):
    n, n_pages = page_table.shape
    pt = page_table.reshape(-1)
    pw = PAGE_SIZE * HA
    n_seq = _SEQ_PER_STEP if n % _SEQ_PER_STEP == 0 else 1
    row = pl.BlockSpec((n_seq, SUBLANES, DA), lambda i, pt: (i, 0, 0))

    def pages(width_shape):
        return [pl.BlockSpec((1, 1) + width_shape,
                             lambda i, pt, sq=sq, j=j: (layer, pt[(i * n_seq + sq) * n_pages + j], 0, 0))
                for sq in range(n_seq) for j in range(n_pages)]

    in_specs = ([row, row, row, pl.BlockSpec((n_seq, SUBLANES, 1), lambda i, pt: (i, 0, 0))]
                + pages((pw, DA)) + pages((pw, DA)) + pages((HA, PAGE_SIZE)))
    grid_spec = pltpu.PrefetchScalarGridSpec(
        num_scalar_prefetch=1, grid=(n // n_seq,), in_specs=in_specs,
        out_specs=pl.BlockSpec((n_seq, SUBLANES, DA), lambda i, pt: (i, 0, 0)))
    n_blk = n_seq * n_pages
    return pl.pallas_call(
        functools.partial(_paged_kernel, n_pages=n_pages, n_seq=n_seq),
        grid_spec=grid_spec,
        out_shape=jax.ShapeDtypeStruct((n, SUBLANES, DA), F32),
        compiler_params=_params(("arbitrary",)),
        name="fox_sample",
    )(pt, q, k_new, v_new, lf_new, *([pool_k] * n_blk), *([pool_v] * n_blk), *([pool_lf] * n_blk))


def _sret_kernel(q_ref, k_ref, v_ref, gate_ref, gn_ref, s0_ref, o_ref, sn_ref):
    q = q_ref[0]
    k = k_ref[0]
    v = v_ref[0]
    hw = DKB * DVB
    lane = lax.broadcasted_iota(jnp.int32, (DKB, hw), 1)
    rowi = lax.broadcasted_iota(jnp.int32, (DKB, hw), 0)
    rep = (lane // DVB == rowi).astype(BF16)
    til = (lane % DVB == rowi).astype(BF16)
    outs = []
    for h in range(HB):
        sl = slice(h * DKB, (h + 1) * DKB)
        g = math.exp(_LOG_G[h])
        sn = g * s0_ref[0, :, h * hw:(h + 1) * hw] + _dot(k[:, sl], rep) * _dot(v[:, sl], til)
        sn_ref[0, :, h * hw:(h + 1) * hw] = sn
        prod = _dot(q[:, sl], rep) * sn
        acc = prod[:, 0:LANES]
        for j in range(1, hw // LANES):
            acc = acc + prod[:, j * LANES:(j + 1) * LANES]
        oh = acc[:, 0:DVB] + acc[:, DVB:LANES]
        mu = jnp.mean(oh, axis=-1, keepdims=True)
        xc = oh - mu
        outs.append(xc * lax.rsqrt(jnp.mean(xc * xc, axis=-1, keepdims=True) + EPS))
    r = jnp.concatenate(outs, axis=1) * gn_ref[...]
    o_ref[0] = (r * gate_ref[0].astype(F32)).astype(BF16)


def _ret_sample(rq, rk, rv, gate, gn, state, layer, nb):
    _, n, _ = rq.shape
    sw = HB * DKB * DVB
    rows = pl.BlockSpec((1, nb, WB), lambda i: (0, i, 0))
    return pl.pallas_call(
        _sret_kernel,
        grid=(n // nb,),
        in_specs=[rows, rows, rows, rows, _const_spec((1, WB)),
                  pl.BlockSpec((1, nb, sw), lambda i: (layer, i, 0))],
        out_specs=[rows, pl.BlockSpec((1, nb, sw), lambda i: (0, i, 0))],
        out_shape=[jax.ShapeDtypeStruct((1, n, WB), BF16), jax.ShapeDtypeStruct((1, n, sw), F32)],
        compiler_params=_params(("arbitrary",)),
        name="ret_sample",
    )(rq, rk, rv, gate, gn, state)


def _sconv_kernel(ctx_ref, u_ref, w_ref, b_ref, lg_ref, lb_ref, o_ref, ns_ref):
    ctx = ctx_ref[0]
    u = u_ref[...]
    w = w_ref[...]
    cv = (jnp.sum(ctx * w[0:CONV_W - 1, :][None], axis=1, keepdims=True)
          + u * w[CONV_W - 1:CONV_W, :][None] + b_ref[...][None])
    o_ref[...] = _silu(_layernorm(cv, lg_ref[...][None], lb_ref[...][None])).astype(BF16)
    ns_ref[0, :, 0:CONV_W - 2, :] = ctx[:, 1:CONV_W - 1, :]
    ns_ref[0, :, CONV_W - 2:CONV_W - 1, :] = u


def _conv_sample(state, u, w, b, lg, lb, layer, nb):
    n = u.shape[0]
    return pl.pallas_call(
        _sconv_kernel,
        grid=(n // nb,),
        in_specs=[pl.BlockSpec((1, nb, CONV_W - 1, DC), lambda i: (layer, i, 0, 0)),
                  pl.BlockSpec((nb, 1, DC), lambda i: (i, 0, 0)),
                  _const_spec((CONV_W, DC)), _const_spec((1, DC)), _const_spec((1, DC)), _const_spec((1, DC))],
        out_specs=[pl.BlockSpec((nb, 1, DC), lambda i: (i, 0, 0)),
                   pl.BlockSpec((1, nb, CONV_W - 1, DC), lambda i: (0, i, 0, 0))],
        out_shape=[jax.ShapeDtypeStruct((n, 1, DC), BF16),
                   jax.ShapeDtypeStruct((1, n, CONV_W - 1, DC), F32)],
        compiler_params=_params(("arbitrary",)),
        name="conv_sample",
    )(state, u, w, b, lg, lb)


def _reorder_w_in(w):
    wt = w.T
    o = 3 * WA
    tail = jnp.pad(wt[o:o + HA], ((0, LANES - HA), (0, 0)))
    return jnp.concatenate([wt[:o], wt[o + HA:], tail], axis=0).astype(BF16)


def _pick(total, pref):
    t = min(total, pref)
    assert total % t == 0
    return t


def kernel(x_prompt, x_sample, cache_k, cache_v, cache_logf, state_ret, state_conv, state_ffn_conv, page_table,
           c_prompt, c_sample, ada_w, ada_b, norms, w_in, b_f, ret_gn_g, conv_w, conv_b, conv_ln_g, conv_ln_b,
           w_o, ffn_up, ffn_conv_w, ffn_conv_b, ffn_down):
    depth = ada_w.shape[0]
    bp, sp, d = x_prompt.shape
    ns, ts, _ = x_sample.shape
    assert ts == 1 and d == D_MODEL
    n_pool = cache_k.shape[1]
    n_pages = page_table.shape[1]

    n_c = ns + bp
    n_c_pad = -(-n_c // SUBLANES) * SUBLANES
    c_all = jnp.pad(jnp.concatenate([c_sample, c_prompt], axis=0), ((0, n_c_pad - n_c), (0, 0)))
    mod_all = _ada_mod(c_all, ada_w, ada_b)

    cos_p, sin_p = _rope_tables(sp, 0, 1)
    cos_s, sin_s = _rope_tables(ns, n_pages * PAGE_SIZE, 0)

    pool_k = cache_k.reshape(depth, n_pool, PAGE_SIZE * HA, DA)
    pool_v = cache_v.reshape(depth, n_pool, PAGE_SIZE * HA, DA)
    pool_lf = cache_logf.transpose(0, 1, 3, 2)
    st_ret = state_ret.reshape(depth, ns, HB * DKB * DVB)
    st_ffn = state_ffn_conv.reshape(depth, ns, 4 * D_FF)

    tm_p = _pick(sp, 512)
    tq = _pick(sp, 1024)
    chunk = _pick(sp, 256)
    nb_s = _pick(ns, 32)

    xp = x_prompt
    xs = x_sample.reshape(1, ns, d)
    p_out, s_out = [], []
    kv_p = kv_s = None
    wo, wup, wdn = w_o.astype(BF16), ffn_up.astype(BF16), ffn_down.astype(BF16)

    def heads_on_rows(a):
        return jnp.pad(a.reshape(ns, HA, DA), ((0, 0), (0, SUBLANES - HA), (0, 0)))

    for l in range(depth):
        w_re = _reorder_w_in(w_in[l])
        bf_pad = jnp.pad(b_f[l], (0, LANES - HA)).reshape(1, LANES)
        n0 = norms[l, 0:1]
        gn = ret_gn_g[l].reshape(1, WB)
        cb = conv_b[l].reshape(1, DC)
        clg = conv_ln_g[l].reshape(1, DC)
        clb = conv_ln_b[l].reshape(1, DC)
        fcb = ffn_conv_b[l].reshape(1, 2 * D_FF)
        mod_s = mod_all[l, 0:ns].reshape(1, ns, 6 * d)
        mod_p = mod_all[l, ns:ns + bp].reshape(bp, 1, 6 * d)

        ko, vo, qa, ka, va, lft, rq, rk, rv, gate, oc, conv_st = _in_proj(
            xp, mod_p, n0, w_re, bf_pad, cos_p, sin_p, tm_p, depth, l, kv_p, conv=(conv_w[l], cb, clg, clb))
        kv_p = (ko, vo)
        oa = _fox_prompt(qa, ka, va, _cumsum(lft), tq)
        ob, s_bd = _ret_prompt(rq, rk, rv, gate, gn, chunk)
        xp, ffn_st = _out_ffn(xp, oa, ob, oc, mod_p, norms[l], wo, wup, ffn_conv_w[l], fcb, wdn, tm_p, layer=l)
        s5 = s_bd.reshape(bp, HB, DKB, HB, DVB)
        ret_st = jnp.stack([s5[:, h, :, h, :] for h in range(HB)], axis=1)
        p_out.append((lft[:, 0:HA, :].transpose(0, 2, 1), ret_st, conv_st, ffn_st))

        ko, vo, qa, ka, va, lft, rq, rk, rv, gate, u = _in_proj(
            xs, mod_s, n0, w_re, bf_pad, cos_s, sin_s, ns, depth, l, kv_s)
        kv_s = (ko, vo)
        lfn = lft[0].T.reshape(ns, SUBLANES, 1)
        oa = _fox_sample(page_table, heads_on_rows(qa.astype(F32)), heads_on_rows(ko[l, 0]),
                         heads_on_rows(vo[l, 0]), lfn, pool_k, pool_v, pool_lf, l)
        oa = oa[:, 0:HA, :].reshape(1, ns, WA).astype(BF16)
        ob, ret_new = _ret_sample(rq, rk, rv, gate, gn, st_ret, l, nb_s)
        oc, conv_new = _conv_sample(state_conv, u.reshape(ns, 1, DC), conv_w[l], cb, clg, clb, l, nb_s)
        xs, ffn_new = _out_ffn(xs, oa, ob, oc.reshape(1, ns, DC), mod_s, norms[l],
                               wo, wup, ffn_conv_w[l], fcb, wdn, ns, state=st_ffn, layer=l)
        s_out.append((lft[0, 0:HA, :].T.reshape(ns, 1, HA), ret_new.reshape(ns, HB, DKB, DVB),
                      conv_new[0], ffn_new.reshape(ns, 2, 2 * D_FF)))

    plf, pret, pconv, pffn = [jnp.stack(a) for a in zip(*p_out)]
    slf, sret, sconv, sffn = [jnp.stack(a) for a in zip(*s_out)]
    pk, pv = [a.reshape(depth, bp, sp, HA, DA) for a in kv_p]
    sk, sv = [a.reshape(depth, ns, 1, HA, DA) for a in kv_s]
    return (xp, xs.reshape(ns, 1, d), pk, pv, plf, pret, pconv, pffn, sk, sv, slf, sret, sconv, sffn)
```
